```python
import functools
import jax, jax.numpy as jnp
from jax import lax
import numpy as np

D_MODEL = 2048
BATCH = 2
SEQ = 16384
DEPTH = 4

HEAD_DIM = 128
N_HEADS = D_MODEL // HEAD_DIM
MIX_WIDTH = N_HEADS * HEAD_DIM
N_SB_HEADS = N_HEADS // 2
N_FOX_HEADS = N_HEADS - N_SB_HEADS
SB_WIDTH = N_SB_HEADS * HEAD_DIM
FOX_WIDTH = N_FOX_HEADS * HEAD_DIM
P_IN = 3 * SB_WIDTH + 4 * FOX_WIDTH + N_FOX_HEADS
D_FF = (11 * D_MODEL) // 4
N_EXPERTS = 8
TOP_K = 2
D_FF_EXPERT = D_MODEL // 2
BLOCK_Q = 128
N_QUERY_CHUNKS = 8
NORM_EPS = 1e-6

kernel_name = "hybrid_stickbreak_fox_moe_trunk"


def rmsnorm(x, g):
    xf = x.astype(jnp.float32)
    y = xf * lax.rsqrt(jnp.mean(xf * xf, axis=-1, keepdims=True) + NORM_EPS)
    return (y * g.astype(jnp.float32)).astype(x.dtype)


def head_rmsnorm(o, g):
    b, s, w = o.shape
    h = w // HEAD_DIM
    return rmsnorm(o.reshape(b, s, h, HEAD_DIM), g.reshape(h, HEAD_DIM)).reshape(b, s, w)


def split_heads(t):
    b, s, w = t.shape
    return t.reshape(b, s, w // HEAD_DIM, HEAD_DIM)


def to_query_blocks(t):
    b, s, h, d = t.shape
    return t.reshape(b, s // BLOCK_Q, BLOCK_Q, h, d).transpose(1, 0, 3, 2, 4)


def from_query_blocks(o):
    nb, b, h, q, d = o.shape
    return o.transpose(1, 0, 3, 2, 4).reshape(b, nb * q, h * d)


def causal_sweep(block_fn, per_block, s_len):
    nb = s_len // BLOCK_Q
    starts = jnp.arange(nb, dtype=jnp.int32) * BLOCK_Q
    n_chunks = min(N_QUERY_CHUNKS, nb)
    outs = []
    for c in range(n_chunks):
        b0 = (c * nb) // n_chunks
        b1 = ((c + 1) * nb) // n_chunks
        xs = tuple(a[b0:b1] for a in per_block) + (starts[b0:b1],)
        outs.append(lax.map(functools.partial(block_fn, key_len=b1 * BLOCK_Q), xs))
    return jnp.concatenate(outs, axis=0)


def stick_breaking_attention(q, k, v):
    s_len = q.shape[1]
    kh = k.transpose(0, 2, 1, 3)
    vh = v.transpose(0, 2, 1, 3)
    scale = HEAD_DIM ** -0.5
    r = jnp.arange(BLOCK_Q)
    tri_incl = (r[:, None] >= r[None, :]).astype(jnp.float32)

    def block(args, key_len):
        q_blk, t0 = args
        nk = key_len // BLOCK_Q
        kk = kh[:, :, :key_len]
        vv = vh[:, :, :key_len]
        z = jnp.einsum('bhqd,bhkd->bhqk', q_blk, kk,
                       preferred_element_type=jnp.float32) * scale
        q_pos = t0 + jnp.arange(BLOCK_Q)
        mask = jnp.arange(key_len)[None, :] < q_pos[:, None]
        sp = jnp.where(mask, jax.nn.softplus(z), 0.0)
        b, h, nq, _ = sp.shape
        sp_b = sp.reshape(b, h, nq, nk, BLOCK_Q)
        within = jnp.einsum('bhqnj,js->bhqns', sp_b, tri_incl)
        nb_idx = jnp.arange(nk)
        strict = (nb_idx[:, None] > nb_idx[None, :]).astype(jnp.float32)
        after = jnp.einsum('bhqn,nm->bhqm', jnp.sum(sp_b, axis=-1), strict)
        rev = (within + after[..., None]).reshape(b, h, nq, key_len)
        w = jnp.exp(jnp.where(mask, z - rev, -jnp.inf))
        return jnp.einsum('bhqk,bhkd->bhqd', w.astype(vv.dtype), vv)

    out = causal_sweep(block, (to_query_blocks(q),), s_len)
    return from_query_blocks(out)


def forgetting_attention(q, k, v, log_f):
    b, s_len, h, _ = q.shape
    nb = s_len // BLOCK_Q
    kh = k.transpose(0, 2, 1, 3)
    vh = v.transpose(0, 2, 1, 3)
    c = lax.cumsum(log_f, axis=1).transpose(0, 2, 1)
    c_blocks = c.reshape(b, h, nb, BLOCK_Q).transpose(2, 0, 1, 3)
    scale = HEAD_DIM ** -0.5

    def block(args, key_len):
        q_blk, c_q, t0 = args
        kk = kh[:, :, :key_len]
        vv = vh[:, :, :key_len]
        logits = jnp.einsum('bhqd,bhkd->bhqk', q_blk, kk,
                            preferred_element_type=jnp.float32) * scale
        logits = logits + c_q[..., :, None] - c[:, :, None, :key_len]
        q_pos = t0 + jnp.arange(BLOCK_Q)
        mask = jnp.arange(key_len)[None, :] <= q_pos[:, None]
        p = jax.nn.softmax(jnp.where(mask, logits, -jnp.inf), axis=-1)
        return jnp.einsum('bhqk,bhkd->bhqd', p.astype(vv.dtype), vv)

    out = causal_sweep(block, (to_query_blocks(q), c_blocks), s_len)
    return from_query_blocks(out)


def hybrid_mixer(a, w_in, b_forget, q_norm, k_norm, sb_out_norm, fox_out_norm, w_out):
    proj = a @ w_in
    offs = [SB_WIDTH, 2 * SB_WIDTH, 3 * SB_WIDTH,
            3 * SB_WIDTH + FOX_WIDTH, 3 * SB_WIDTH + 2 * FOX_WIDTH,
            3 * SB_WIDTH + 3 * FOX_WIDTH, 3 * SB_WIDTH + 4 * FOX_WIDTH]
    sb_q, sb_k, sb_v, fx_q, fx_k, fx_v, fx_g, fx_f = jnp.split(proj, offs, axis=-1)
    sb_o = stick_breaking_attention(split_heads(sb_q), split_heads(sb_k), split_heads(sb_v))
    sb_o = head_rmsnorm(sb_o, sb_out_norm)
    fq = rmsnorm(split_heads(fx_q), q_norm)
    fk = rmsnorm(split_heads(fx_k), k_norm)
    log_f = jax.nn.log_sigmoid((fx_f + b_forget).astype(jnp.float32))
    fox_o = forgetting_attention(fq, fk, split_heads(fx_v), log_f)
    fox_o = head_rmsnorm(fox_o, fox_out_norm) * jax.nn.sigmoid(fx_g)
    return jnp.concatenate([sb_o, fox_o], axis=-1) @ w_out


def swiglu(h, wg, wu, wd):
    return (jax.nn.silu(h @ wg) * (h @ wu)) @ wd


def moe_swiglu(h, router_w, wg, wu, wd):
    b, s, d = h.shape
    tokens = h.reshape(b * s, d)
    probs = jax.nn.softmax((tokens @ router_w).astype(jnp.float32), axis=-1)
    top_p, top_i = lax.top_k(probs, TOP_K)
    top_p = top_p / jnp.sum(top_p, axis=-1, keepdims=True)
    gates = jnp.sum(jax.nn.one_hot(top_i, N_EXPERTS, dtype=jnp.float32) * top_p[..., None],
                    axis=1).astype(tokens.dtype)
    y = jnp.zeros_like(tokens)
    for e in range(N_EXPERTS):
        y = y + gates[:, e:e + 1] * swiglu(tokens, wg[e], wu[e], wd[e])
    return y.reshape(b, s, d)


def setup_inputs(seed: int = 0) -> dict:
    key = jax.random.key(seed)
    ks = jax.random.split(key, 20)
    n_dense = (DEPTH + 1) // 2
    n_moe = DEPTH // 2
    nrm = jax.random.normal
    f32 = jnp.float32

    def gain(k, shape):
        return 1.0 + 0.02 * nrm(k, shape, f32)

    return {
        "x": nrm(ks[0], (BATCH, SEQ, D_MODEL), f32),
        "attn_norm": gain(ks[1], (DEPTH, D_MODEL)),
        "w_in": nrm(ks[2], (DEPTH, D_MODEL, P_IN), f32) * D_MODEL ** -0.5,
        "b_forget": 0.1 * nrm(ks[3], (DEPTH, N_FOX_HEADS), f32),
        "fox_q_norm": gain(ks[4], (DEPTH, HEAD_DIM)),
        "fox_k_norm": gain(ks[5], (DEPTH, HEAD_DIM)),
        "sb_out_norm": gain(ks[6], (DEPTH, SB_WIDTH)),
        "fox_out_norm": gain(ks[7], (DEPTH, FOX_WIDTH)),
        "w_out": nrm(ks[8], (DEPTH, MIX_WIDTH, D_MODEL), f32) * MIX_WIDTH ** -0.5,
        "ffn_norm": gain(ks[9], (DEPTH, D_MODEL)),
        "dense_w_gate": nrm(ks[10], (n_dense, D_MODEL, D_FF), f32) * D_MODEL ** -0.5,
        "dense_w_up": nrm(ks[11], (n_dense, D_MODEL, D_FF), f32) * D_MODEL ** -0.5,
        "dense_w_down": nrm(ks[12], (n_dense, D_FF, D_MODEL), f32) * D_FF ** -0.5,
        "router_w": nrm(ks[13], (n_moe, D_MODEL, N_EXPERTS), f32) * D_MODEL ** -0.5,
        "moe_w_gate": nrm(ks[14], (n_moe, N_EXPERTS, D_MODEL, D_FF_EXPERT), f32) * D_MODEL ** -0.5,
        "moe_w_up": nrm(ks[15], (n_moe, N_EXPERTS, D_MODEL, D_FF_EXPERT), f32) * D_MODEL ** -0.5,
        "moe_w_down": nrm(ks[16], (n_moe, N_EXPERTS, D_FF_EXPERT, D_MODEL), f32) * D_FF_EXPERT ** -0.5,
        "final_norm": gain(ks[17], (D_MODEL,)),
    }


def reference(x, attn_norm, w_in, b_forget, fox_q_norm, fox_k_norm, sb_out_norm,
              fox_out_norm, w_out, ffn_norm, dense_w_gate, dense_w_up, dense_w_down,
              router_w, moe_w_gate, moe_w_up, moe_w_down, final_norm):
    h = x
    for layer in range(DEPTH):
        a = rmsnorm(h, attn_norm[layer])
        h = h + hybrid_mixer(a, w_in[layer], b_forget[layer], fox_q_norm[layer],
                             fox_k_norm[layer], sb_out_norm[layer], fox_out_norm[layer],
                             w_out[layer])
        m = rmsnorm(h, ffn_norm[layer])
        i = layer // 2
        if layer % 2 == 0:
            h = h + swiglu(m, dense_w_gate[i], dense_w_up[i], dense_w_down[i])
        else:
            h = h + moe_swiglu(m, router_w[i], moe_w_gate[i], moe_w_up[i], moe_w_down[i])
    return rmsnorm(h, final_norm)
```

```python
import functools

import jax
import jax.numpy as jnp
from jax import lax
from jax.experimental import pallas as pl
from jax.experimental.pallas import tpu as pltpu

F32 = jnp.float32
BF16 = jnp.bfloat16

HEAD_DIM = 128
LANES = 128
N_EXPERTS_TOP_K = 2
NORM_EPS = 1e-6
EXP_ZERO_BELOW = -104.0
NEG_BIG = -1e30
VMEM_LIMIT_BYTES = 56 * 1024 * 1024

ATTN_BLOCK = 256
CUMSUM_BLOCK = 256
ROW_BLOCK = 512
FF_BLOCK = 512


def _params(*sem):
    return pltpu.CompilerParams(dimension_semantics=sem, vmem_limit_bytes=VMEM_LIMIT_BYTES)


def _row_rms_scale(x):
    return lax.rsqrt(jnp.mean(x * x, axis=-1, keepdims=True) + NORM_EPS)


def _in_proj_kernel(x_ref, g_ref, w_ref, wf_ref, qk_gain_ref, p_ref, f_ref, a_scr, *, scale):
    j = pl.program_id(1)

    @pl.when(j == 0)
    def _():
        x = x_ref[...]
        a = (x * _row_rms_scale(x)) * g_ref[...]
        a_scr[...] = a.astype(BF16)
        f_ref[...] = jnp.dot(a_scr[...], wf_ref[...], preferred_element_type=F32)

    acc = jnp.dot(a_scr[...], w_ref[...], preferred_element_type=F32)
    heads = acc.shape[1] // HEAD_DIM

    @pl.when(j == 0)
    def _():
        p_ref[...] = (acc * scale).astype(BF16)

    @pl.when(jnp.logical_or(j == 3, j == 4))
    def _():
        gain = qk_gain_ref[j - 3]
        for hh in range(heads):
            t = acc[:, hh * HEAD_DIM:(hh + 1) * HEAD_DIM]
            y = (t * _row_rms_scale(t)) * gain
            p_ref[:, hh * HEAD_DIM:(hh + 1) * HEAD_DIM] = y.astype(BF16)

    @pl.when(jnp.logical_and(j != 0, jnp.logical_and(j != 3, j != 4)))
    def _():
        p_ref[...] = acc.astype(BF16)


def _in_proj(h, g, w, wf, qk_gain, seg_width):
    n, d = h.shape
    bm = min(ROW_BLOCK, n)
    bn = seg_width
    n_tiles = w.shape[1] // bn
    kern = functools.partial(_in_proj_kernel, scale=HEAD_DIM ** -0.5)
    return pl.pallas_call(
        kern,
        grid=(n // bm, n_tiles),
        in_specs=[
            pl.BlockSpec((bm, d), lambda i, j: (i, 0)),
            pl.BlockSpec((1, d), lambda i, j: (0, 0)),
            pl.BlockSpec((d, bn), lambda i, j: (0, j)),
            pl.BlockSpec((d, LANES), lambda i, j: (0, 0)),
            pl.BlockSpec((2, 1, HEAD_DIM), lambda i, j: (0, 0, 0)),
        ],
        out_specs=[
            pl.BlockSpec((bm, bn), lambda i, j: (i, j)),
            pl.BlockSpec((bm, LANES), lambda i, j: (i, 0)),
        ],
        out_shape=[
            jax.ShapeDtypeStruct((n, w.shape[1]), BF16),
            jax.ShapeDtypeStruct((n, LANES), F32),
        ],
        scratch_shapes=[pltpu.VMEM((bm, d), BF16)],
        compiler_params=_params("parallel", "arbitrary"),
        name="in_proj",
    )(h, g, w, wf, qk_gain)


def _split3(x):
    hi = x.astype(BF16)
    r = x - hi.astype(F32)
    mid = r.astype(BF16)
    lo = (r - mid.astype(F32)).astype(BF16)
    return hi, mid, lo


def _forget_cumsum_kernel(f_ref, b_ref, c_ref, ct_ref, carry_scr, *, n_heads):
    @pl.when(pl.program_id(1) == 0)
    def _():
        carry_scr[...] = jnp.zeros_like(carry_scr)

    x = f_ref[...] + b_ref[...]
    log_f = jnp.minimum(x, 0.0) - jnp.log(1.0 + jnp.exp(-jnp.abs(x)))
    bc = x.shape[0]
    row = lax.broadcasted_iota(jnp.int32, (bc, bc), 0)
    col = lax.broadcasted_iota(jnp.int32, (bc, bc), 1)
    lower = jnp.where(col <= row, 1.0, 0.0).astype(BF16)
    hi, mid, lo = _split3(log_f)
    cum = (jnp.dot(lower, hi, preferred_element_type=F32)
           + jnp.dot(lower, mid, preferred_element_type=F32)
           + jnp.dot(lower, lo, preferred_element_type=F32))
    c = cum + carry_scr[...]
    c_ref[...] = c
    carry_scr[...] = c[bc - 1:bc, :]
    ct_ref[0] = c.T[:n_heads, :]


def _forget_cumsum(f, b_pad, batch, n_heads):
    n = f.shape[0]
    s = n // batch
    bc = min(CUMSUM_BLOCK, s)
    nblk = s // bc
    kern = functools.partial(_forget_cumsum_kernel, n_heads=n_heads)
    return pl.pallas_call(
        kern,
        grid=(batch, nblk),
        in_specs=[
            pl.BlockSpec((bc, LANES), lambda b, t: (b * nblk + t, 0)),
            pl.BlockSpec((1, LANES), lambda b, t: (0, 0)),
        ],
        out_specs=[
            pl.BlockSpec((bc, LANES), lambda b, t: (b * nblk + t, 0)),
            pl.BlockSpec((1, n_heads, bc), lambda b, t: (b, 0, t)),
        ],
        out_shape=[
            jax.ShapeDtypeStruct((n, LANES), F32),
            jax.ShapeDtypeStruct((batch, n_heads, s), F32),
        ],
        scratch_shapes=[pltpu.VMEM((1, LANES), F32)],
        compiler_params=_params("arbitrary", "arbitrary"),
        name="forget_cumsum",
    )(f, b_pad)


def _max_key_norm(k_ref, out_scr, chunk):
    n_chunks = k_ref.shape[0] // chunk

    def body(c, mx):
        kb = k_ref[pl.ds(pl.multiple_of(c * chunk, chunk), chunk), :].astype(F32)
        n2 = jnp.sum(kb * kb, axis=-1, keepdims=True)
        return jnp.maximum(mx, jnp.max(n2, axis=0, keepdims=True))

    mx = lax.fori_loop(0, n_chunks, body, jnp.zeros((1, 1), F32))
    out_scr[...] = jnp.broadcast_to(jnp.sqrt(mx), out_scr.shape)


def _qk(q, kb):
    return lax.dot_general(q, kb, (((1,), (1,)), ((), ())), preferred_element_type=F32)


def _head_rmsnorm(o, gain):
    return (o * _row_rms_scale(o)) * gain


def _sb_attn_kernel(q_ref, k_ref, v_ref, tri_ref, gain_ref, o_ref,
                    acc_scr, carry_scr, kmax_scr, *, blk, exit_below):
    qi = pl.program_id(2)

    @pl.when(qi == 0)
    def _():
        _max_key_norm(k_ref, kmax_scr, blk)

    q = q_ref[...]
    qf = q.astype(F32)
    q_norm = jnp.sqrt(jnp.sum(qf * qf, axis=-1, keepdims=True))
    z_bound = q_norm * kmax_scr[0:1, 0:1] * 1.001 + 1e-3
    tri = tri_ref[...]
    acc_scr[...] = jnp.zeros_like(acc_scr)
    carry_scr[...] = jnp.zeros_like(carry_scr)

    def visit(j, diagonal):
        start = pl.multiple_of(j * blk, blk)
        kb = k_ref[pl.ds(start, blk), :]
        vb = v_ref[pl.ds(start, blk), :]
        z = _qk(q, kb)
        sp = jnp.maximum(z, 0.0) + jnp.log(1.0 + jnp.exp(-jnp.abs(z)))
        if diagonal:
            r = lax.broadcasted_iota(jnp.int32, z.shape, 0)
            c = lax.broadcasted_iota(jnp.int32, z.shape, 1)
            causal = c < r
            sp = jnp.where(causal, sp, 0.0)
        within = jnp.dot(sp.astype(BF16), tri, preferred_element_type=F32)
        rev = within + carry_scr[...]
        w = jnp.exp(z - rev)
        if diagonal:
            w = jnp.where(causal, w, 0.0)
        acc_scr[...] += jnp.dot(w.astype(BF16), vb, preferred_element_type=F32)
        carry_scr[...] += jnp.sum(sp, axis=-1, keepdims=True)

    def more_needed():
        return (jnp.max(z_bound - carry_scr[...]) >= exit_below).astype(jnp.int32)

    visit(qi, True)

    def cond(state):
        j, go = state
        return jnp.logical_and(j >= 0, go > 0)

    def body(state):
        j, _ = state
        visit(j, False)
        return j - 1, more_needed()

    lax.while_loop(cond, body, (qi - 1, more_needed()))
    o_ref[...] = _head_rmsnorm(acc_scr[...], gain_ref[0]).astype(BF16)


def _sb_attn(p, tri, gain, batch, n_heads, exit_below):
    n = p.shape[0]
    s = n // batch
    blk = min(ATTN_BLOCK, s)
    nq = s // blk
    kern = functools.partial(_sb_attn_kernel, blk=blk, exit_below=exit_below)
    return pl.pallas_call(
        kern,
        grid=(batch, n_heads, nq),
        in_specs=[
            pl.BlockSpec((blk, HEAD_DIM), lambda b, h, i: (b * nq + i, h)),
            pl.BlockSpec((s, HEAD_DIM), lambda b, h, i: (b, n_heads + h)),
            pl.BlockSpec((s, HEAD_DIM), lambda b, h, i: (b, 2 * n_heads + h)),
            pl.BlockSpec((blk, blk), lambda b, h, i: (0, 0)),
            pl.BlockSpec((1, 1, HEAD_DIM), lambda b, h, i: (h, 0, 0)),
        ],
        out_specs=pl.BlockSpec((blk, HEAD_DIM), lambda b, h, i: (b * nq + i, h)),
        out_shape=jax.ShapeDtypeStruct((n, n_heads * HEAD_DIM), BF16),
        scratch_shapes=[
            pltpu.VMEM((blk, HEAD_DIM), F32),
            pltpu.VMEM((blk, 1), F32),
            pltpu.VMEM((8, LANES), F32),
        ],
        compiler_params=_params("arbitrary", "arbitrary", "arbitrary"),
        name="sb_attn",
    )(p, p, p, tri, gain)


def _fox_attn_kernel(q_ref, k_ref, v_ref, gate_ref, c_ref, ct_ref, gain_ref, o_ref,
                     acc_scr, m_scr, l_scr, kmax_scr, *, blk, exit_below):
    h = pl.program_id(1)
    qi = pl.program_id(2)

    @pl.when(qi == 0)
    def _():
        _max_key_norm(k_ref, kmax_scr, blk)

    q = q_ref[...]
    qf = q.astype(F32)
    q_norm = jnp.sqrt(jnp.sum(qf * qf, axis=-1, keepdims=True))
    qk_bound = q_norm * kmax_scr[0:1, 0:1] * 1.001 + 1.0
    c_blk = c_ref[...]
    lane = lax.broadcasted_iota(jnp.int32, c_blk.shape, 1)
    c_q = jnp.sum(jnp.where(lane == h, c_blk, 0.0), axis=-1, keepdims=True)
    acc_scr[...] = jnp.zeros_like(acc_scr)
    l_scr[...] = jnp.zeros_like(l_scr)
    m_scr[...] = jnp.full_like(m_scr, NEG_BIG)

    def visit(j, diagonal):
        start = pl.multiple_of(j * blk, blk)
        kb = k_ref[pl.ds(start, blk), :]
        vb = v_ref[pl.ds(start, blk), :]
        c_k = ct_ref[0, j]
        logit = _qk(q, kb) + (c_q - c_k)
        if diagonal:
            r = lax.broadcasted_iota(jnp.int32, logit.shape, 0)
            c = lax.broadcasted_iota(jnp.int32, logit.shape, 1)
            logit = jnp.where(c <= r, logit, -jnp.inf)
        m_old = m_scr[...]
        m_new = jnp.maximum(m_old, jnp.max(logit, axis=-1, keepdims=True))
        alpha = jnp.exp(m_old - m_new)
        pr = jnp.exp(logit - m_new)
        l_scr[...] = alpha * l_scr[...] + jnp.sum(pr, axis=-1, keepdims=True)
        acc_scr[...] = alpha * acc_scr[...] + jnp.dot(pr.astype(BF16), vb,
                                                      preferred_element_type=F32)
        m_scr[...] = m_new
        bound = qk_bound + (c_q - c_k[0:1, 0:1])
        return (jnp.max(bound - m_new) >= exit_below).astype(jnp.int32)

    go0 = visit(qi, True)

    def cond(state):
        j, go = state
        return jnp.logical_and(j >= 0, go > 0)

    def body(state):
        j, _ = state
        return j - 1, visit(j, False)

    lax.while_loop(cond, body, (qi - 1, go0))
    o = acc_scr[...] / l_scr[...]
    y = _head_rmsnorm(o, gain_ref[0])
    gate = gate_ref[...].astype(F32)
    o_ref[...] = (y * (1.0 / (1.0 + jnp.exp(-gate)))).astype(BF16)


def _fox_attn(p, c, ct4, gain, batch, n_sb_heads, n_heads, exit_below):
    n = p.shape[0]
    s = n // batch
    blk = min(ATTN_BLOCK, s)
    nq = s // blk
    base = 3 * n_sb_heads
    kern = functools.partial(_fox_attn_kernel, blk=blk, exit_below=exit_below)
    return pl.pallas_call(
        kern,
        grid=(batch, n_heads, nq),
        in_specs=[
            pl.BlockSpec((blk, HEAD_DIM), lambda b, h, i: (b * nq + i, base + h)),
            pl.BlockSpec((s, HEAD_DIM), lambda b, h, i: (b, base + n_heads + h)),
            pl.BlockSpec((s, HEAD_DIM), lambda b, h, i: (b, base + 2 * n_heads + h)),
            pl.BlockSpec((blk, HEAD_DIM), lambda b, h, i: (b * nq + i, base + 3 * n_heads + h)),
            pl.BlockSpec((blk, LANES), lambda b, h, i: (b * nq + i, 0)),
            pl.BlockSpec((1, nq, 1, blk), lambda b, h, i: (b * n_heads + h, 0, 0, 0)),
            pl.BlockSpec((1, 1, HEAD_DIM), lambda b, h, i: (h, 0, 0)),
        ],
        out_specs=pl.BlockSpec((blk, HEAD_DIM), lambda b, h, i: (b * nq + i, h)),
        out_shape=jax.ShapeDtypeStruct((n, n_heads * HEAD_DIM), BF16),
        scratch_shapes=[
            pltpu.VMEM((blk, HEAD_DIM), F32),
            pltpu.VMEM((blk, 1), F32),
            pltpu.VMEM((blk, 1), F32),
            pltpu.VMEM((8, LANES), F32),
        ],
        compiler_params=_params("arbitrary", "arbitrary", "arbitrary"),
        name="fox_attn",
    )(p, p, p, p, c, ct4, gain)


def _out_proj_kernel(h_ref, a_ref, b_ref, wa_ref, wb_ref, o_ref):
    o_ref[...] = (h_ref[...]
                  + jnp.dot(a_ref[...], wa_ref[...], preferred_element_type=F32)
                  + jnp.dot(b_ref[...], wb_ref[...], preferred_element_type=F32))


def _out_proj(h, o_sb, o_fox, w_sb, w_fox):
    n, d = h.shape
    bm = min(ROW_BLOCK, n)
    return pl.pallas_call(
        _out_proj_kernel,
        grid=(n // bm,),
        in_specs=[
            pl.BlockSpec((bm, d), lambda i: (i, 0)),
            pl.BlockSpec((bm, o_sb.shape[1]), lambda i: (i, 0)),
            pl.BlockSpec((bm, o_fox.shape[1]), lambda i: (i, 0)),
            pl.BlockSpec(w_sb.shape, lambda i: (0, 0)),
            pl.BlockSpec(w_fox.shape, lambda i: (0, 0)),
        ],
        out_specs=pl.BlockSpec((bm, d), lambda i: (i, 0)),
        out_shape=jax.ShapeDtypeStruct((n, d), F32),
        input_output_aliases={0: 0},
        compiler_params=_params("parallel"),
        name="out_proj",
    )(h, o_sb, o_fox, w_sb, w_fox)


def _router_gates(m32, m_hi, wr_hi_ref, wr_lo_ref, n_experts):
    m_lo = (m32 - m_hi.astype(F32)).astype(BF16)
    logits = (jnp.dot(m_hi, wr_hi_ref[...], preferred_element_type=F32)
              + jnp.dot(m_lo, wr_hi_ref[...], preferred_element_type=F32)
              + jnp.dot(m_hi, wr_lo_ref[...], preferred_element_type=F32))
    lane = lax.broadcasted_iota(jnp.int32, logits.shape, 1)
    valid = lane < n_experts
    logits = jnp.where(valid, logits, -jnp.inf)
    e = jnp.exp(logits - jnp.max(logits, axis=-1, keepdims=True))
    probs = e / jnp.sum(e, axis=-1, keepdims=True)
    p1 = jnp.max(probs, axis=-1, keepdims=True)
    i1 = jnp.min(jnp.where(probs == p1, lane, LANES), axis=-1, keepdims=True)
    rest = jnp.where(lane == i1, -1.0, probs)
    p2 = jnp.max(rest, axis=-1, keepdims=True)
    i2 = jnp.min(jnp.where(rest == p2, lane, LANES), axis=-1, keepdims=True)
    denom = p1 + p2
    return jnp.where(lane == i1, p1 / denom, 0.0) + jnp.where(lane == i2, p2 / denom, 0.0)


def _ffn_kernel(*refs, routed, final_norm, n_experts, tiles_per_expert):
    if routed:
        (x_ref, g_ref, wg_ref, wu_ref, wd_ref, wr_hi_ref, wr_lo_ref, fin_ref,
         o_ref, m_scr, acc_scr, gates_scr) = refs
    else:
        x_ref, g_ref, wg_ref, wu_ref, wd_ref, fin_ref, o_ref, m_scr, acc_scr = refs
    j = pl.program_id(1)

    @pl.when(j == 0)
    def _():
        x = x_ref[...]
        m32 = (x * _row_rms_scale(x)) * g_ref[...]
        m_hi = m32.astype(BF16)
        m_scr[...] = m_hi
        acc_scr[...] = jnp.zeros_like(acc_scr)
        if routed:
            gates_scr[...] = _router_gates(m32, m_hi, wr_hi_ref, wr_lo_ref, n_experts)

    m = m_scr[...]
    gate = jnp.dot(m, wg_ref[...], preferred_element_type=F32)
    up = jnp.dot(m, wu_ref[...], preferred_element_type=F32)
    act = ((gate * (1.0 / (1.0 + jnp.exp(-gate)))) * up).astype(BF16)
    y = jnp.dot(act, wd_ref[...], preferred_element_type=F32)
    if routed:
        gates = gates_scr[...]
        lane = lax.broadcasted_iota(jnp.int32, gates.shape, 1)
        expert = j // tiles_per_expert
        y = jnp.sum(jnp.where(lane == expert, gates, 0.0), axis=-1, keepdims=True) * y
    acc_scr[...] += y

    @pl.when(j == pl.num_programs(1) - 1)
    def _():
        out = x_ref[...] + acc_scr[...]
        if final_norm:
            out = (out * _row_rms_scale(out)) * fin_ref[...]
        o_ref[...] = out


def _ffn(h, g, wg, wu, wd, fin, router=None, final_norm=False):
    n, d = h.shape
    n_experts, _, f = wg.shape
    bm = min(ROW_BLOCK, n)
    bf = min(FF_BLOCK, f)
    tiles_per_expert = f // bf
    routed = router is not None
    kern = functools.partial(_ffn_kernel, routed=routed, final_norm=final_norm,
                             n_experts=n_experts, tiles_per_expert=tiles_per_expert)
    in_specs = [
        pl.BlockSpec((bm, d), lambda i, j: (i, 0)),
        pl.BlockSpec((1, d), lambda i, j: (0, 0)),
        pl.BlockSpec((None, d, bf), lambda i, j: (j // tiles_per_expert, 0, j % tiles_per_expert)),
        pl.BlockSpec((None, d, bf), lambda i, j: (j // tiles_per_expert, 0, j % tiles_per_expert)),
        pl.BlockSpec((None, bf, d), lambda i, j: (j // tiles_per_expert, j % tiles_per_expert, 0)),
    ]
    args = [h, g, wg, wu, wd]
    scratch = [pltpu.VMEM((bm, d), BF16), pltpu.VMEM((bm, d), F32)]
    if routed:
        in_specs += [pl.BlockSpec((d, LANES), lambda i, j: (0, 0)),
                     pl.BlockSpec((d, LANES), lambda i, j: (0, 0))]
        args += list(router)
        scratch.append(pltpu.VMEM((bm, LANES), F32))
    in_specs.append(pl.BlockSpec((1, d), lambda i, j: (0, 0)))
    args.append(fin)
    return pl.pallas_call(
        kern,
        grid=(n // bm, n_experts * tiles_per_expert),
        in_specs=in_specs,
        out_specs=pl.BlockSpec((bm, d), lambda i, j: (i, 0)),
        out_shape=jax.ShapeDtypeStruct((n, d), F32),
        scratch_shapes=scratch,
        input_output_aliases={0: 0},
        compiler_params=_params("parallel", "arbitrary"),
        name="moe_ffn" if routed else "dense_ffn",
    )(*args)


def _pad_lanes(a):
    return jnp.pad(a, ((0, 0),) * (a.ndim - 1) + ((0, LANES - a.shape[-1]),))


def kernel(x, attn_norm, w_in, b_forget, fox_q_norm, fox_k_norm, sb_out_norm, fox_out_norm,
           w_out, ffn_norm, dense_w_gate, dense_w_up, dense_w_down, router_w, moe_w_gate,
           moe_w_up, moe_w_down, final_norm):
    batch, seq, d = x.shape
    depth = w_in.shape[0]
    n_fox = b_forget.shape[1]
    fox_width = n_fox * HEAD_DIM
    sb_width = sb_out_norm.shape[1]
    n_sb = sb_width // HEAD_DIM
    assert sb_width == fox_width, "projection column tiles assume equal head-group widths"
    assert w_in.shape[2] == 3 * sb_width + 4 * fox_width + n_fox
    n_main = 3 * sb_width + 4 * fox_width
    blk = min(ATTN_BLOCK, seq)
    assert seq % blk == 0 and n_fox <= 8 and depth >= 1
    scale = HEAD_DIM ** -0.5

    r = lax.broadcasted_iota(jnp.int32, (blk, blk), 0)
    c = lax.broadcasted_iota(jnp.int32, (blk, blk), 1)
    tri = (r >= c).astype(BF16)

    h = x.reshape(batch * seq, d)
    for layer in range(depth):
        w_main = w_in[layer, :, :n_main].astype(BF16)
        w_forget = _pad_lanes(w_in[layer, :, n_main:]).astype(BF16)
        qk_gain = jnp.stack([fox_q_norm[layer] * scale, fox_k_norm[layer]]).reshape(2, 1, HEAD_DIM)
        p, f = _in_proj(h, attn_norm[layer].reshape(1, d), w_main, w_forget, qk_gain, sb_width)
        c_tok, c_seq = _forget_cumsum(f, _pad_lanes(b_forget[layer].reshape(1, n_fox)), batch, n_fox)
        ct4 = c_seq.reshape(batch * n_fox, seq // blk, 1, blk)
        o_sb = _sb_attn(p, tri, sb_out_norm[layer].reshape(n_sb, 1, HEAD_DIM), batch, n_sb,
                        EXP_ZERO_BELOW)
        o_fox = _fox_attn(p, c_tok, ct4, fox_out_norm[layer].reshape(n_fox, 1, HEAD_DIM), batch,
                          n_sb, n_fox, EXP_ZERO_BELOW)
        w_o = w_out[layer].astype(BF16)
        h = _out_proj(h, o_sb, o_fox, w_o[:sb_width], w_o[sb_width:])
        i = layer // 2
        last = layer == depth - 1
        fin = final_norm.reshape(1, d)
        g_ffn = ffn_norm[layer].reshape(1, d)
        if layer % 2 == 0:
            h = _ffn(h, g_ffn, dense_w_gate[i][None].astype(BF16), dense_w_up[i][None].astype(BF16),
                     dense_w_down[i][None].astype(BF16), fin, final_norm=last)
        else:
            wr = _pad_lanes(router_w[i])
            wr_hi = wr.astype(BF16)
            wr_lo = (wr - wr_hi.astype(F32)).astype(BF16)
            h = _ffn(h, g_ffn, moe_w_gate[i].astype(BF16), moe_w_up[i].astype(BF16),
                     moe_w_down[i].astype(BF16), fin, router=(wr_hi, wr_lo), final_norm=last)
    return h.reshape(batch, seq, d)
```

```python
import functools

import jax
import jax.numpy as jnp
from jax import lax
from jax.experimental import pallas as pl
from jax.experimental.pallas import tpu as pltpu

F32 = jnp.float32
BF16 = jnp.bfloat16

HEAD_DIM = 128
LANES = 128
N_EXPERTS_TOP_K = 2
NORM_EPS = 1e-6
EXP_ZERO_BELOW = -104.0
NEG_BIG = -1e30
VMEM_LIMIT_BYTES = 56 * 1024 * 1024

ATTN_BLOCK = 256
ATTN_SUB_BLOCKS = 4
CUMSUM_BLOCK = 256
ROW_BLOCK = 512
IN_PROJ_ROW_BLOCK = 1024
FF_BLOCK = 512


def _params(*sem):
    return pltpu.CompilerParams(dimension_semantics=sem, vmem_limit_bytes=VMEM_LIMIT_BYTES)


def _row_rms_scale(x):
    return lax.rsqrt(jnp.mean(x * x, axis=-1, keepdims=True) + NORM_EPS)


def _in_proj_kernel(x_ref, g_ref, w_ref, wf_ref, qk_gain_ref, p_ref, f_ref, a_scr, *, scale):
    j = pl.program_id(1)

    @pl.when(j == 0)
    def _():
        x = x_ref[...]
        a = (x * _row_rms_scale(x)) * g_ref[...]
        a_scr[...] = a.astype(BF16)
        f_ref[...] = jnp.dot(a_scr[...], wf_ref[...], preferred_element_type=F32)

    acc = jnp.dot(a_scr[...], w_ref[...], preferred_element_type=F32)
    heads = acc.shape[1] // HEAD_DIM

    @pl.when(j == 0)
    def _():
        p_ref[...] = (acc * scale).astype(BF16)

    @pl.when(jnp.logical_or(j == 3, j == 4))
    def _():
        gain = qk_gain_ref[j - 3]
        for hh in range(heads):
            t = acc[:, hh * HEAD_DIM:(hh + 1) * HEAD_DIM]
            y = (t * _row_rms_scale(t)) * gain
            p_ref[:, hh * HEAD_DIM:(hh + 1) * HEAD_DIM] = y.astype(BF16)

    @pl.when(jnp.logical_and(j != 0, jnp.logical_and(j != 3, j != 4)))
    def _():
        p_ref[...] = acc.astype(BF16)


def _in_proj(h, g, w, wf, qk_gain, seg_width):
    n, d = h.shape
    bm = min(IN_PROJ_ROW_BLOCK, n)
    bn = seg_width
    n_tiles = w.shape[1] // bn
    kern = functools.partial(_in_proj_kernel, scale=HEAD_DIM ** -0.5)
    return pl.pallas_call(
        kern,
        grid=(n // bm, n_tiles),
        in_specs=[
            pl.BlockSpec((bm, d), lambda i, j: (i, 0)),
            pl.BlockSpec((1, d), lambda i, j: (0, 0)),
            pl.BlockSpec((d, bn), lambda i, j: (0, j)),
            pl.BlockSpec((d, LANES), lambda i, j: (0, 0)),
            pl.BlockSpec((2, 1, HEAD_DIM), lambda i, j: (0, 0, 0)),
        ],
        out_specs=[
            pl.BlockSpec((bm, bn), lambda i, j: (i, j)),
            pl.BlockSpec((bm, LANES), lambda i, j: (i, 0)),
        ],
        out_shape=[
            jax.ShapeDtypeStruct((n, w.shape[1]), BF16),
            jax.ShapeDtypeStruct((n, LANES), F32),
        ],
        scratch_shapes=[pltpu.VMEM((bm, d), BF16)],
        compiler_params=_params("parallel", "arbitrary"),
        name="in_proj",
    )(h, g, w, wf, qk_gain)


def _split3(x):
    hi = x.astype(BF16)
    r = x - hi.astype(F32)
    mid = r.astype(BF16)
    lo = (r - mid.astype(F32)).astype(BF16)
    return hi, mid, lo


def _forget_cumsum_kernel(f_ref, b_ref, c_ref, ct_ref, carry_scr, *, n_heads):
    @pl.when(pl.program_id(1) == 0)
    def _():
        carry_scr[...] = jnp.zeros_like(carry_scr)

    x = f_ref[...] + b_ref[...]
    log_f = jnp.minimum(x, 0.0) - jnp.log(1.0 + jnp.exp(-jnp.abs(x)))
    bc = x.shape[0]
    row = lax.broadcasted_iota(jnp.int32, (bc, bc), 0)
    col = lax.broadcasted_iota(jnp.int32, (bc, bc), 1)
    lower = jnp.where(col <= row, 1.0, 0.0).astype(BF16)
    hi, mid, lo = _split3(log_f)
    cum = (jnp.dot(lower, hi, preferred_element_type=F32)
           + jnp.dot(lower, mid, preferred_element_type=F32)
           + jnp.dot(lower, lo, preferred_element_type=F32))
    c = cum + carry_scr[...]
    c_ref[...] = c
    carry_scr[...] = c[bc - 1:bc, :]
    ct_ref[0] = c.T[:n_heads, :]


def _forget_cumsum(f, b_pad, batch, n_heads):
    n = f.shape[0]
    s = n // batch
    bc = min(CUMSUM_BLOCK, s)
    nblk = s // bc
    kern = functools.partial(_forget_cumsum_kernel, n_heads=n_heads)
    return pl.pallas_call(
        kern,
        grid=(batch, nblk),
        in_specs=[
            pl.BlockSpec((bc, LANES), lambda b, t: (b * nblk + t, 0)),
            pl.BlockSpec((1, LANES), lambda b, t: (0, 0)),
        ],
        out_specs=[
            pl.BlockSpec((bc, LANES), lambda b, t: (b * nblk + t, 0)),
            pl.BlockSpec((1, n_heads, bc), lambda b, t: (b, 0, t)),
        ],
        out_shape=[
            jax.ShapeDtypeStruct((n, LANES), F32),
            jax.ShapeDtypeStruct((batch, n_heads, s), F32),
        ],
        scratch_shapes=[pltpu.VMEM((1, LANES), F32)],
        compiler_params=_params("arbitrary", "arbitrary"),
        name="forget_cumsum",
    )(f, b_pad)


def _max_key_norm(k_ref, out_scr, chunk):
    n_chunks = k_ref.shape[0] // chunk

    def body(c, mx):
        kb = k_ref[pl.ds(pl.multiple_of(c * chunk, chunk), chunk), :].astype(F32)
        n2 = jnp.sum(kb * kb, axis=-1, keepdims=True)
        return jnp.maximum(mx, jnp.max(n2, axis=0, keepdims=True))

    mx = lax.fori_loop(0, n_chunks, body, jnp.zeros((1, 1), F32))
    out_scr[...] = jnp.broadcast_to(jnp.sqrt(mx), out_scr.shape)


def _qk(q, kb):
    return lax.dot_general(q, kb, (((1,), (1,)), ((), ())), preferred_element_type=F32)


def _head_rmsnorm(o, gain):
    return (o * _row_rms_scale(o)) * gain


def _softplus(z):
    return jnp.maximum(z, 0.0) + jnp.log(1.0 + jnp.exp(-jnp.abs(z)))


def _sb_attn_kernel(q_ref, k_ref, v_ref, tri_ref, gain_ref, o_ref,
                    acc_scr, carry_scr, kmax_scr, *, blk, sub, exit_below):
    qi = pl.program_id(2)

    @pl.when(qi == 0)
    def _():
        _max_key_norm(k_ref, kmax_scr, blk)

    tri = tri_ref[...]
    kmax = kmax_scr[0:1, 0:1]
    r_idx = lax.broadcasted_iota(jnp.int32, (blk, blk), 0)
    c_idx = lax.broadcasted_iota(jnp.int32, (blk, blk), 1)
    causal = c_idx < r_idx

    def kv_block(j):
        start = pl.multiple_of(j * blk, blk)
        return k_ref[pl.ds(start, blk), :], v_ref[pl.ds(start, blk), :]

    z_bounds = []
    for r in range(sub):
        rows = slice(r * blk, (r + 1) * blk)
        g = qi * sub + r
        q = q_ref[rows, :]
        qf = q.astype(F32)
        q_norm = jnp.sqrt(jnp.sum(qf * qf, axis=-1, keepdims=True))
        z_bounds.append(q_norm * kmax * 1.001 + 1e-3)
        kd, vd = kv_block(g)
        z = _qk(q, kd)
        sp = jnp.where(causal, _softplus(z), 0.0)
        within = jnp.dot(sp.astype(BF16), tri, preferred_element_type=F32)
        w = jnp.where(causal, jnp.exp(z - within), 0.0)
        acc = jnp.dot(w.astype(BF16), vd, preferred_element_type=F32)
        carry = jnp.sum(sp, axis=-1, keepdims=True)
        kp, vp = kv_block(jnp.maximum(g - 1, 0))
        z = _qk(q, kp)
        sp = _softplus(z)
        within = jnp.dot(sp.astype(BF16), tri, preferred_element_type=F32)
        rev_base = carry if r > 0 else carry + jnp.where(g > 0, 0.0, -NEG_BIG)
        w = jnp.exp(z - (within + rev_base))
        acc_scr[rows, :] = acc + jnp.dot(w.astype(BF16), vp, preferred_element_type=F32)
        carry_scr[rows, :] = carry + jnp.sum(sp, axis=-1, keepdims=True)

    for r in range(sub):
        rows = slice(r * blk, (r + 1) * blk)
        z_bound = z_bounds[r]

        def more_needed(rows=rows, z_bound=z_bound):
            return (jnp.max(z_bound - carry_scr[rows, :]) >= exit_below).astype(jnp.int32)

        def cond(state):
            j, go = state
            return jnp.logical_and(j >= 0, go > 0)

        def body(state, rows=rows, more_needed=more_needed):
            j, _ = state
            q = q_ref[rows, :]
            kb, vb = kv_block(j)
            z = _qk(q, kb)
            sp = _softplus(z)
            within = jnp.dot(sp.astype(BF16), tri, preferred_element_type=F32)
            w = jnp.exp(z - (within + carry_scr[rows, :]))
            acc_scr[rows, :] += jnp.dot(w.astype(BF16), vb, preferred_element_type=F32)
            carry_scr[rows, :] += jnp.sum(sp, axis=-1, keepdims=True)
            return j - 1, more_needed()

        lax.while_loop(cond, body, (qi * sub + r - 2, more_needed()))

    o_ref[...] = _head_rmsnorm(acc_scr[...], gain_ref[0]).astype(BF16)


def _attn_tiling(s):
    blk = min(ATTN_BLOCK, s)
    sub = min(ATTN_SUB_BLOCKS, s // blk)
    return blk, sub, s // (blk * sub)


def _sb_attn(p, tri, gain, batch, n_heads, exit_below):
    n = p.shape[0]
    s = n // batch
    blk, sub, nq = _attn_tiling(s)
    tq = blk * sub
    kern = functools.partial(_sb_attn_kernel, blk=blk, sub=sub, exit_below=exit_below)
    return pl.pallas_call(
        kern,
        grid=(batch, n_heads, nq),
        in_specs=[
            pl.BlockSpec((tq, HEAD_DIM), lambda b, h, i: (b * nq + i, h)),
            pl.BlockSpec((s, HEAD_DIM), lambda b, h, i: (b, n_heads + h)),
            pl.BlockSpec((s, HEAD_DIM), lambda b, h, i: (b, 2 * n_heads + h)),
            pl.BlockSpec((blk, blk), lambda b, h, i: (0, 0)),
            pl.BlockSpec((1, 1, HEAD_DIM), lambda b, h, i: (h, 0, 0)),
        ],
        out_specs=pl.BlockSpec((tq, HEAD_DIM), lambda b, h, i: (b * nq + i, h)),
        out_shape=jax.ShapeDtypeStruct((n, n_heads * HEAD_DIM), BF16),
        scratch_shapes=[
            pltpu.VMEM((tq, HEAD_DIM), F32),
            pltpu.VMEM((tq, 1), F32),
            pltpu.VMEM((8, LANES), F32),
        ],
        compiler_params=_params("arbitrary", "arbitrary", "arbitrary"),
        name="sb_attn",
    )(p, p, p, tri, gain)


def _fox_attn_kernel(q_ref, k_ref, v_ref, gate_ref, c_ref, ct_ref, gain_ref, o_ref,
                     acc_scr, m_scr, l_scr, kmax_scr, *, blk, sub, exit_below):
    h = pl.program_id(1)
    qi = pl.program_id(2)

    @pl.when(qi == 0)
    def _():
        _max_key_norm(k_ref, kmax_scr, blk)

    kmax = kmax_scr[0:1, 0:1]
    r_idx = lax.broadcasted_iota(jnp.int32, (blk, blk), 0)
    c_idx = lax.broadcasted_iota(jnp.int32, (blk, blk), 1)
    causal = c_idx <= r_idx
    lane = lax.broadcasted_iota(jnp.int32, (blk, LANES), 1)

    def kv_block(j):
        start = pl.multiple_of(j * blk, blk)
        return k_ref[pl.ds(start, blk), :], v_ref[pl.ds(start, blk), :]

    bounds = []
    for r in range(sub):
        rows = slice(r * blk, (r + 1) * blk)
        g = qi * sub + r
        gp = jnp.maximum(g - 1, 0)
        q = q_ref[rows, :]
        qf = q.astype(F32)
        q_norm = jnp.sqrt(jnp.sum(qf * qf, axis=-1, keepdims=True))
        c_q = jnp.sum(jnp.where(lane == h, c_ref[rows, :], 0.0), axis=-1, keepdims=True)
        bounds.append((q_norm * kmax * 1.001 + 1.0) + c_q)
        kd, vd = kv_block(g)
        kp, vp = kv_block(gp)
        c_kd = ct_ref[0, g]
        c_kp = ct_ref[0, gp]
        logit_d = jnp.where(causal, _qk(q, kd) + (c_q - c_kd), -jnp.inf)
        c_q_prev = c_q if r > 0 else c_q + jnp.where(g > 0, 0.0, NEG_BIG)
        logit_p = _qk(q, kp) + (c_q_prev - c_kp)
        m = jnp.maximum(jnp.max(logit_d, axis=-1, keepdims=True),
                        jnp.max(logit_p, axis=-1, keepdims=True))
        pd = jnp.exp(logit_d - m)
        pp = jnp.exp(logit_p - m)
        m_scr[rows, :] = m
        l_scr[rows, :] = jnp.sum(pd, axis=-1, keepdims=True) + jnp.sum(pp, axis=-1, keepdims=True)
        acc_scr[rows, :] = (jnp.dot(pd.astype(BF16), vd, preferred_element_type=F32)
                            + jnp.dot(pp.astype(BF16), vp, preferred_element_type=F32))

    for r in range(sub):
        rows = slice(r * blk, (r + 1) * blk)
        bound_q = bounds[r]
        g = qi * sub + r

        def more_needed(j_done, rows=rows, bound_q=bound_q):
            first_c = ct_ref[0, j_done][0:1, 0:1]
            return (jnp.max(bound_q - first_c - m_scr[rows, :]) >= exit_below).astype(jnp.int32)

        def cond(state):
            j, go = state
            return jnp.logical_and(j >= 0, go > 0)

        def body(state, rows=rows, more_needed=more_needed):
            j, _ = state
            q = q_ref[rows, :]
            kb, vb = kv_block(j)
            c_q = jnp.sum(jnp.where(lane == h, c_ref[rows, :], 0.0), axis=-1, keepdims=True)
            logit = _qk(q, kb) + (c_q - ct_ref[0, j])
            m_old = m_scr[rows, :]
            m_new = jnp.maximum(m_old, jnp.max(logit, axis=-1, keepdims=True))
            alpha = jnp.exp(m_old - m_new)
            pr = jnp.exp(logit - m_new)
            l_scr[rows, :] = alpha * l_scr[rows, :] + jnp.sum(pr, axis=-1, keepdims=True)
            acc_scr[rows, :] = alpha * acc_scr[rows, :] + jnp.dot(pr.astype(BF16), vb,
                                                                  preferred_element_type=F32)
            m_scr[rows, :] = m_new
            return j - 1, more_needed(j)

        lax.while_loop(cond, body, (g - 2, more_needed(jnp.maximum(g - 1, 0))))

    o = acc_scr[...] / l_scr[...]
    y = _head_rmsnorm(o, gain_ref[0])
    gate = gate_ref[...].astype(F32)
    o_ref[...] = (y * (1.0 / (1.0 + jnp.exp(-gate)))).astype(BF16)


def _fox_attn(p, c, ct4, gain, batch, n_sb_heads, n_heads, exit_below):
    n = p.shape[0]
    s = n // batch
    blk, sub, nq = _attn_tiling(s)
    tq = blk * sub
    base = 3 * n_sb_heads
    kern = functools.partial(_fox_attn_kernel, blk=blk, sub=sub, exit_below=exit_below)
    return pl.pallas_call(
        kern,
        grid=(batch, n_heads, nq),
        in_specs=[
            pl.BlockSpec((tq, HEAD_DIM), lambda b, h, i: (b * nq + i, base + h)),
            pl.BlockSpec((s, HEAD_DIM), lambda b, h, i: (b, base + n_heads + h)),
            pl.BlockSpec((s, HEAD_DIM), lambda b, h, i: (b, base + 2 * n_heads + h)),
            pl.BlockSpec((tq, HEAD_DIM), lambda b, h, i: (b * nq + i, base + 3 * n_heads + h)),
            pl.BlockSpec((tq, LANES), lambda b, h, i: (b * nq + i, 0)),
            pl.BlockSpec((1, s // blk, 1, blk), lambda b, h, i: (b * n_heads + h, 0, 0, 0)),
            pl.BlockSpec((1, 1, HEAD_DIM), lambda b, h, i: (h, 0, 0)),
        ],
        out_specs=pl.BlockSpec((tq, HEAD_DIM), lambda b, h, i: (b * nq + i, h)),
        out_shape=jax.ShapeDtypeStruct((n, n_heads * HEAD_DIM), BF16),
        scratch_shapes=[
            pltpu.VMEM((tq, HEAD_DIM), F32),
            pltpu.VMEM((tq, 1), F32),
            pltpu.VMEM((tq, 1), F32),
            pltpu.VMEM((8, LANES), F32),
        ],
        compiler_params=_params("arbitrary", "arbitrary", "arbitrary"),
        name="fox_attn",
    )(p, p, p, p, c, ct4, gain)


def _out_proj_kernel(h_ref, a_ref, b_ref, wa_ref, wb_ref, o_ref):
    o_ref[...] = (h_ref[...]
                  + jnp.dot(a_ref[...], wa_ref[...], preferred_element_type=F32)
                  + jnp.dot(b_ref[...], wb_ref[...], preferred_element_type=F32))


def _out_proj(h, o_sb, o_fox, w_sb, w_fox):
    n, d = h.shape
    bm = min(ROW_BLOCK, n)
    return pl.pallas_call(
        _out_proj_kernel,
        grid=(n // bm,),
        in_specs=[
            pl.BlockSpec((bm, d), lambda i: (i, 0)),
            pl.BlockSpec((bm, o_sb.shape[1]), lambda i: (i, 0)),
            pl.BlockSpec((bm, o_fox.shape[1]), lambda i: (i, 0)),
            pl.BlockSpec(w_sb.shape, lambda i: (0, 0)),
            pl.BlockSpec(w_fox.shape, lambda i: (0, 0)),
        ],
        out_specs=pl.BlockSpec((bm, d), lambda i: (i, 0)),
        out_shape=jax.ShapeDtypeStruct((n, d), F32),
        compiler_params=_params("parallel"),
        name="out_proj",
    )(h, o_sb, o_fox, w_sb, w_fox)


def _router_gates(m32, m_hi, wr_hi_ref, wr_lo_ref, n_experts):
    m_lo = (m32 - m_hi.astype(F32)).astype(BF16)
    logits = (jnp.dot(m_hi, wr_hi_ref[...], preferred_element_type=F32)
              + jnp.dot(m_lo, wr_hi_ref[...], preferred_element_type=F32)
              + jnp.dot(m_hi, wr_lo_ref[...], preferred_element_type=F32))
    lane = lax.broadcasted_iota(jnp.int32, logits.shape, 1)
    valid = lane < n_experts
    logits = jnp.where(valid, logits, -jnp.inf)
    e = jnp.exp(logits - jnp.max(logits, axis=-1, keepdims=True))
    probs = e / jnp.sum(e, axis=-1, keepdims=True)
    p1 = jnp.max(probs, axis=-1, keepdims=True)
    i1 = jnp.min(jnp.where(probs == p1, lane, LANES), axis=-1, keepdims=True)
    rest = jnp.where(lane == i1, -1.0, probs)
    p2 = jnp.max(rest, axis=-1, keepdims=True)
    i2 = jnp.min(jnp.where(rest == p2, lane, LANES), axis=-1, keepdims=True)
    denom = p1 + p2
    return jnp.where(lane == i1, p1 / denom, 0.0) + jnp.where(lane == i2, p2 / denom, 0.0)


def _ffn_kernel(*refs, routed, final_norm, n_experts, tiles_per_expert):
    if routed:
        (x_ref, g_ref, wg_ref, wu_ref, wd_ref, wr_hi_ref, wr_lo_ref, fin_ref,
         o_ref, m_scr, acc_scr, gates_scr) = refs
    else:
        x_ref, g_ref, wg_ref, wu_ref, wd_ref, fin_ref, o_ref, m_scr, acc_scr = refs
    j = pl.program_id(1)

    @pl.when(j == 0)
    def _():
        x = x_ref[...]
        m32 = (x * _row_rms_scale(x)) * g_ref[...]
        m_hi = m32.astype(BF16)
        m_scr[...] = m_hi
        acc_scr[...] = jnp.zeros_like(acc_scr)
        if routed:
            gates_scr[...] = _router_gates(m32, m_hi, wr_hi_ref, wr_lo_ref, n_experts)

    m = m_scr[...]
    gate = jnp.dot(m, wg_ref[...], preferred_element_type=F32)
    up = jnp.dot(m, wu_ref[...], preferred_element_type=F32)
    act = ((gate * (1.0 / (1.0 + jnp.exp(-gate)))) * up).astype(BF16)
    y = jnp.dot(act, wd_ref[...], preferred_element_type=F32)
    if routed:
        gates = gates_scr[...]
        lane = lax.broadcasted_iota(jnp.int32, gates.shape, 1)
        expert = j // tiles_per_expert
        y = jnp.sum(jnp.where(lane == expert, gates, 0.0), axis=-1, keepdims=True) * y
    acc_scr[...] += y

    @pl.when(j == pl.num_programs(1) - 1)
    def _():
        out = x_ref[...] + acc_scr[...]
        if final_norm:
            out = (out * _row_rms_scale(out)) * fin_ref[...]
        o_ref[...] = out


def _ffn(h, g, wg, wu, wd, fin, router=None, final_norm=False):
    n, d = h.shape
    n_experts, _, f = wg.shape
    bm = min(ROW_BLOCK, n)
    bf = min(FF_BLOCK, f)
    tiles_per_expert = f // bf
    routed = router is not None
    kern = functools.partial(_ffn_kernel, routed=routed, final_norm=final_norm,
                             n_experts=n_experts, tiles_per_expert=tiles_per_expert)
    in_specs = [
        pl.BlockSpec((bm, d), lambda i, j: (i, 0)),
        pl.BlockSpec((1, d), lambda i, j: (0, 0)),
        pl.BlockSpec((None, d, bf), lambda i, j: (j // tiles_per_expert, 0, j % tiles_per_expert)),
        pl.BlockSpec((None, d, bf), lambda i, j: (j // tiles_per_expert, 0, j % tiles_per_expert)),
        pl.BlockSpec((None, bf, d), lambda i, j: (j // tiles_per_expert, j % tiles_per_expert, 0)),
    ]
    args = [h, g, wg, wu, wd]
    scratch = [pltpu.VMEM((bm, d), BF16), pltpu.VMEM((bm, d), F32)]
    if routed:
        in_specs += [pl.BlockSpec((d, LANES), lambda i, j: (0, 0)),
                     pl.BlockSpec((d, LANES), lambda i, j: (0, 0))]
        args += list(router)
        scratch.append(pltpu.VMEM((bm, LANES), F32))
    in_specs.append(pl.BlockSpec((1, d), lambda i, j: (0, 0)))
    args.append(fin)
    return pl.pallas_call(
        kern,
        grid=(n // bm, n_experts * tiles_per_expert),
        in_specs=in_specs,
        out_specs=pl.BlockSpec((bm, d), lambda i, j: (i, 0)),
        out_shape=jax.ShapeDtypeStruct((n, d), F32),
        scratch_shapes=scratch,
        compiler_params=_params("parallel", "arbitrary"),
        name="moe_ffn" if routed else "dense_ffn",
    )(*args)


def _pad_lanes(a):
    return jnp.pad(a, ((0, 0),) * (a.ndim - 1) + ((0, LANES - a.shape[-1]),))


def kernel(x, attn_norm, w_in, b_forget, fox_q_norm, fox_k_norm, sb_out_norm, fox_out_norm,
           w_out, ffn_norm, dense_w_gate, dense_w_up, dense_w_down, router_w, moe_w_gate,
           moe_w_up, moe_w_down, final_norm):
    batch, seq, d = x.shape
    depth = w_in.shape[0]
    n_fox = b_forget.shape[1]
    fox_width = n_fox * HEAD_DIM
    sb_width = sb_out_norm.shape[1]
    n_sb = sb_width // HEAD_DIM
    assert sb_width == fox_width, "projection column tiles assume equal head-group widths"
    assert w_in.shape[2] == 3 * sb_width + 4 * fox_width + n_fox
    n_main = 3 * sb_width + 4 * fox_width
    blk = min(ATTN_BLOCK, seq)
    assert seq % blk == 0 and n_fox <= 8 and depth >= 1
    scale = HEAD_DIM ** -0.5

    r = lax.broadcasted_iota(jnp.int32, (blk, blk), 0)
    c = lax.broadcasted_iota(jnp.int32, (blk, blk), 1)
    tri = (r >= c).astype(BF16)

    h = x.reshape(batch * seq, d)
    for layer in range(depth):
        w_main = w_in[layer, :, :n_main].astype(BF16)
        w_forget = _pad_lanes(w_in[layer, :, n_main:]).astype(BF16)
        qk_gain = jnp.stack([fox_q_norm[layer] * scale, fox_k_norm[layer]]).reshape(2, 1, HEAD_DIM)
        p, f = _in_proj(h, attn_norm[layer].reshape(1, d), w_main, w_forget, qk_gain, sb_width)
        c_tok, c_seq = _forget_cumsum(f, _pad_lanes(b_forget[layer].reshape(1, n_fox)), batch, n_fox)
        ct4 = c_seq.reshape(batch * n_fox, seq // blk, 1, blk)
        o_sb = _sb_attn(p, tri, sb_out_norm[layer].reshape(n_sb, 1, HEAD_DIM), batch, n_sb,
                        EXP_ZERO_BELOW)
        o_fox = _fox_attn(p, c_tok, ct4, fox_out_norm[layer].reshape(n_fox, 1, HEAD_DIM), batch,
                          n_sb, n_fox, EXP_ZERO_BELOW)
        w_o = w_out[layer].astype(BF16)
        h = _out_proj(h, o_sb, o_fox, w_o[:sb_width], w_o[sb_width:])
        i = layer // 2
        last = layer == depth - 1
        fin = final_norm.reshape(1, d)
        g_ffn = ffn_norm[layer].reshape(1, d)
        if layer % 2 == 0:
            h = _ffn(h, g_ffn, dense_w_gate[i][None].astype(BF16), dense_w_up[i][None].astype(BF16),
                     dense_w_down[i][None].astype(BF16), fin, final_norm=last)
        else:
            wr = _pad_lanes(router_w[i])
            wr_hi = wr.astype(BF16)
            wr_lo = (wr - wr_hi.astype(F32)).astype(BF16)
            h = _ffn(h, g_ffn, moe_w_gate[i].astype(BF16), moe_w_up[i].astype(BF16),
                     moe_w_down[i].astype(BF16), fin, router=(wr_hi, wr_lo), final_norm=last)
    return h.reshape(batch, seq, d)
```

```python
import functools

import jax
import jax.numpy as jnp
from jax import lax
from jax.experimental import pallas as pl
from jax.experimental.pallas import tpu as pltpu

F32 = jnp.float32
BF16 = jnp.bfloat16

HEAD_DIM = 128
LANES = 128
N_EXPERTS_TOP_K = 2
NORM_EPS = 1e-6
EXP_ZERO_BELOW = -104.0
NEG_BIG = -1e30
VMEM_LIMIT_BYTES = 56 * 1024 * 1024

ATTN_BLOCK = 256
ATTN_SUB_BLOCKS = 4
CUMSUM_BLOCK = 256
ROW_BLOCK = 512
IN_PROJ_ROW_BLOCK = 1024
FF_BLOCK = 512
EXPERT_TILE_ROWS = 512
DISPATCH_TOKENS = 1024
COMBINE_TOKENS = 256
ROW_COPY_UNROLL = 8


def _params(*sem):
    return pltpu.CompilerParams(dimension_semantics=sem, vmem_limit_bytes=VMEM_LIMIT_BYTES)


def _row_rms_scale(x):
    return lax.rsqrt(jnp.mean(x * x, axis=-1, keepdims=True) + NORM_EPS)


def _in_proj_kernel(x_ref, g_ref, w_ref, wf_ref, qk_gain_ref, p_ref, f_ref, a_scr, *, scale):
    j = pl.program_id(1)

    @pl.when(j == 0)
    def _():
        x = x_ref[...]
        a = (x * _row_rms_scale(x)) * g_ref[...]
        a_scr[...] = a.astype(BF16)
        f_ref[...] = jnp.dot(a_scr[...], wf_ref[...], preferred_element_type=F32)

    acc = jnp.dot(a_scr[...], w_ref[...], preferred_element_type=F32)
    heads = acc.shape[1] // HEAD_DIM

    @pl.when(j == 0)
    def _():
        p_ref[...] = (acc * scale).astype(BF16)

    @pl.when(jnp.logical_or(j == 3, j == 4))
    def _():
        gain = qk_gain_ref[j - 3]
        for hh in range(heads):
            t = acc[:, hh * HEAD_DIM:(hh + 1) * HEAD_DIM]
            y = (t * _row_rms_scale(t)) * gain
            p_ref[:, hh * HEAD_DIM:(hh + 1) * HEAD_DIM] = y.astype(BF16)

    @pl.when(jnp.logical_and(j != 0, jnp.logical_and(j != 3, j != 4)))
    def _():
        p_ref[...] = acc.astype(BF16)


def _in_proj(h, g, w, wf, qk_gain, seg_width):
    n, d = h.shape
    bm = min(IN_PROJ_ROW_BLOCK, n)
    bn = seg_width
    n_tiles = w.shape[1] // bn
    kern = functools.partial(_in_proj_kernel, scale=HEAD_DIM ** -0.5)
    return pl.pallas_call(
        kern,
        grid=(n // bm, n_tiles),
        in_specs=[
            pl.BlockSpec((bm, d), lambda i, j: (i, 0)),
            pl.BlockSpec((1, d), lambda i, j: (0, 0)),
            pl.BlockSpec((d, bn), lambda i, j: (0, j)),
            pl.BlockSpec((d, LANES), lambda i, j: (0, 0)),
            pl.BlockSpec((2, 1, HEAD_DIM), lambda i, j: (0, 0, 0)),
        ],
        out_specs=[
            pl.BlockSpec((bm, bn), lambda i, j: (i, j)),
            pl.BlockSpec((bm, LANES), lambda i, j: (i, 0)),
        ],
        out_shape=[
            jax.ShapeDtypeStruct((n, w.shape[1]), BF16),
            jax.ShapeDtypeStruct((n, LANES), F32),
        ],
        scratch_shapes=[pltpu.VMEM((bm, d), BF16)],
        compiler_params=_params("parallel", "arbitrary"),
        name="in_proj",
    )(h, g, w, wf, qk_gain)


def _split3(x):
    hi = x.astype(BF16)
    r = x - hi.astype(F32)
    mid = r.astype(BF16)
    lo = (r - mid.astype(F32)).astype(BF16)
    return hi, mid, lo


def _forget_cumsum_kernel(f_ref, b_ref, c_ref, ct_ref, carry_scr, *, n_heads):
    @pl.when(pl.program_id(1) == 0)
    def _():
        carry_scr[...] = jnp.zeros_like(carry_scr)

    x = f_ref[...] + b_ref[...]
    log_f = jnp.minimum(x, 0.0) - jnp.log(1.0 + jnp.exp(-jnp.abs(x)))
    bc = x.shape[0]
    row = lax.broadcasted_iota(jnp.int32, (bc, bc), 0)
    col = lax.broadcasted_iota(jnp.int32, (bc, bc), 1)
    lower = jnp.where(col <= row, 1.0, 0.0).astype(BF16)
    hi, mid, lo = _split3(log_f)
    cum = (jnp.dot(lower, hi, preferred_element_type=F32)
           + jnp.dot(lower, mid, preferred_element_type=F32)
           + jnp.dot(lower, lo, preferred_element_type=F32))
    c = cum + carry_scr[...]
    c_ref[...] = c
    carry_scr[...] = c[bc - 1:bc, :]
    ct_ref[0] = c.T[:n_heads, :]


def _forget_cumsum(f, b_pad, batch, n_heads):
    n = f.shape[0]
    s = n // batch
    bc = min(CUMSUM_BLOCK, s)
    nblk = s // bc
    kern = functools.partial(_forget_cumsum_kernel, n_heads=n_heads)
    return pl.pallas_call(
        kern,
        grid=(batch, nblk),
        in_specs=[
            pl.BlockSpec((bc, LANES), lambda b, t: (b * nblk + t, 0)),
            pl.BlockSpec((1, LANES), lambda b, t: (0, 0)),
        ],
        out_specs=[
            pl.BlockSpec((bc, LANES), lambda b, t: (b * nblk + t, 0)),
            pl.BlockSpec((1, n_heads, bc), lambda b, t: (b, 0, t)),
        ],
        out_shape=[
            jax.ShapeDtypeStruct((n, LANES), F32),
            jax.ShapeDtypeStruct((batch, n_heads, s), F32),
        ],
        scratch_shapes=[pltpu.VMEM((1, LANES), F32)],
        compiler_params=_params("arbitrary", "arbitrary"),
        name="forget_cumsum",
    )(f, b_pad)


def _max_key_norm(k_ref, out_scr, chunk):
    n_chunks = k_ref.shape[0] // chunk

    def body(c, mx):
        kb = k_ref[pl.ds(pl.multiple_of(c * chunk, chunk), chunk), :].astype(F32)
        n2 = jnp.sum(kb * kb, axis=-1, keepdims=True)
        return jnp.maximum(mx, jnp.max(n2, axis=0, keepdims=True))

    mx = lax.fori_loop(0, n_chunks, body, jnp.zeros((1, 1), F32))
    out_scr[...] = jnp.broadcast_to(jnp.sqrt(mx), out_scr.shape)


def _qk(q, kb):
    return lax.dot_general(q, kb, (((1,), (1,)), ((), ())), preferred_element_type=F32)


def _head_rmsnorm(o, gain):
    return (o * _row_rms_scale(o)) * gain


def _softplus(z):
    return jnp.maximum(z, 0.0) + jnp.log(1.0 + jnp.exp(-jnp.abs(z)))


def _sb_attn_kernel(q_ref, k_ref, v_ref, tri_ref, gain_ref, o_ref,
                    acc_scr, carry_scr, kmax_scr, *, blk, sub, exit_below):
    qi = pl.program_id(2)

    @pl.when(qi == 0)
    def _():
        _max_key_norm(k_ref, kmax_scr, blk)

    tri = tri_ref[...]
    kmax = kmax_scr[0:1, 0:1]
    r_idx = lax.broadcasted_iota(jnp.int32, (blk, blk), 0)
    c_idx = lax.broadcasted_iota(jnp.int32, (blk, blk), 1)
    causal = c_idx < r_idx

    def kv_block(j):
        start = pl.multiple_of(j * blk, blk)
        return k_ref[pl.ds(start, blk), :], v_ref[pl.ds(start, blk), :]

    z_bounds, kvs, zs, sps, withins = [], [], [], [], []
    for r in range(sub):
        g = qi * sub + r
        q = q_ref[r * blk:(r + 1) * blk, :]
        qf = q.astype(F32)
        q_norm = jnp.sqrt(jnp.sum(qf * qf, axis=-1, keepdims=True))
        z_bounds.append(q_norm * kmax * 1.001 + 1e-3)
        kd, vd = kv_block(g)
        kp, vp = kv_block(jnp.maximum(g - 1, 0))
        kvs.append((vd, vp))
        zs.append((_qk(q, kd), _qk(q, kp)))
    for r in range(sub):
        z_d, z_p = zs[r]
        sps.append((jnp.where(causal, _softplus(z_d), 0.0), _softplus(z_p)))
    for r in range(sub):
        sp_d, sp_p = sps[r]
        withins.append((jnp.dot(sp_d.astype(BF16), tri, preferred_element_type=F32),
                        jnp.dot(sp_p.astype(BF16), tri, preferred_element_type=F32)))
    for r in range(sub):
        rows = slice(r * blk, (r + 1) * blk)
        (z_d, z_p), (sp_d, sp_p), (in_d, in_p), (vd, vp) = zs[r], sps[r], withins[r], kvs[r]
        carry = jnp.sum(sp_d, axis=-1, keepdims=True)
        w_d = jnp.where(causal, jnp.exp(z_d - in_d), 0.0)
        rev_base = carry if r > 0 else carry + jnp.where(qi > 0, 0.0, -NEG_BIG)
        w_p = jnp.exp(z_p - (in_p + rev_base))
        acc_scr[rows, :] = (jnp.dot(w_d.astype(BF16), vd, preferred_element_type=F32)
                            + jnp.dot(w_p.astype(BF16), vp, preferred_element_type=F32))
        carry_scr[rows, :] = carry + jnp.sum(sp_p, axis=-1, keepdims=True)

    for r in range(sub):
        rows = slice(r * blk, (r + 1) * blk)
        z_bound = z_bounds[r]

        def more_needed(rows=rows, z_bound=z_bound):
            return (jnp.max(z_bound - carry_scr[rows, :]) >= exit_below).astype(jnp.int32)

        def cond(state):
            j, go = state
            return jnp.logical_and(j >= 0, go > 0)

        def body(state, rows=rows, more_needed=more_needed):
            j, _ = state
            q = q_ref[rows, :]
            kb, vb = kv_block(j)
            z = _qk(q, kb)
            sp = _softplus(z)
            within = jnp.dot(sp.astype(BF16), tri, preferred_element_type=F32)
            w = jnp.exp(z - (within + carry_scr[rows, :]))
            acc_scr[rows, :] += jnp.dot(w.astype(BF16), vb, preferred_element_type=F32)
            carry_scr[rows, :] += jnp.sum(sp, axis=-1, keepdims=True)
            return j - 1, more_needed()

        lax.while_loop(cond, body, (qi * sub + r - 2, more_needed()))

    o_ref[...] = _head_rmsnorm(acc_scr[...], gain_ref[0]).astype(BF16)


def _attn_tiling(s):
    blk = min(ATTN_BLOCK, s)
    sub = min(ATTN_SUB_BLOCKS, s // blk)
    return blk, sub, s // (blk * sub)


def _sb_attn(p, tri, gain, batch, n_heads, exit_below):
    n = p.shape[0]
    s = n // batch
    blk, sub, nq = _attn_tiling(s)
    tq = blk * sub
    kern = functools.partial(_sb_attn_kernel, blk=blk, sub=sub, exit_below=exit_below)
    return pl.pallas_call(
        kern,
        grid=(batch, n_heads, nq),
        in_specs=[
            pl.BlockSpec((tq, HEAD_DIM), lambda b, h, i: (b * nq + i, h)),
            pl.BlockSpec((s, HEAD_DIM), lambda b, h, i: (b, n_heads + h)),
            pl.BlockSpec((s, HEAD_DIM), lambda b, h, i: (b, 2 * n_heads + h)),
            pl.BlockSpec((blk, blk), lambda b, h, i: (0, 0)),
            pl.BlockSpec((1, 1, HEAD_DIM), lambda b, h, i: (h, 0, 0)),
        ],
        out_specs=pl.BlockSpec((tq, HEAD_DIM), lambda b, h, i: (b * nq + i, h)),
        out_shape=jax.ShapeDtypeStruct((n, n_heads * HEAD_DIM), BF16),
        scratch_shapes=[
            pltpu.VMEM((tq, HEAD_DIM), F32),
            pltpu.VMEM((tq, 1), F32),
            pltpu.VMEM((8, LANES), F32),
        ],
        compiler_params=_params("arbitrary", "arbitrary", "arbitrary"),
        name="sb_attn",
    )(p, p, p, tri, gain)


def _fox_attn_kernel(q_ref, k_ref, v_ref, gate_ref, c_ref, ct_ref, gain_ref, o_ref,
                     acc_scr, m_scr, l_scr, kmax_scr, *, blk, sub, exit_below):
    h = pl.program_id(1)
    qi = pl.program_id(2)

    @pl.when(qi == 0)
    def _():
        _max_key_norm(k_ref, kmax_scr, blk)

    kmax = kmax_scr[0:1, 0:1]
    r_idx = lax.broadcasted_iota(jnp.int32, (blk, blk), 0)
    c_idx = lax.broadcasted_iota(jnp.int32, (blk, blk), 1)
    causal = c_idx <= r_idx
    lane = lax.broadcasted_iota(jnp.int32, (blk, LANES), 1)

    def kv_block(j):
        start = pl.multiple_of(j * blk, blk)
        return k_ref[pl.ds(start, blk), :], v_ref[pl.ds(start, blk), :]

    bounds = []
    for r in range(sub):
        rows = slice(r * blk, (r + 1) * blk)
        g = qi * sub + r
        gp = jnp.maximum(g - 1, 0)
        q = q_ref[rows, :]
        qf = q.astype(F32)
        q_norm = jnp.sqrt(jnp.sum(qf * qf, axis=-1, keepdims=True))
        c_q = jnp.sum(jnp.where(lane == h, c_ref[rows, :], 0.0), axis=-1, keepdims=True)
        bounds.append((q_norm * kmax * 1.001 + 1.0) + c_q)
        kd, vd = kv_block(g)
        kp, vp = kv_block(gp)
        c_kd = ct_ref[0, g]
        c_kp = ct_ref[0, gp]
        logit_d = jnp.where(causal, _qk(q, kd) + (c_q - c_kd), -jnp.inf)
        c_q_prev = c_q if r > 0 else c_q + jnp.where(g > 0, 0.0, NEG_BIG)
        logit_p = _qk(q, kp) + (c_q_prev - c_kp)
        m = jnp.maximum(jnp.max(logit_d, axis=-1, keepdims=True),
                        jnp.max(logit_p, axis=-1, keepdims=True))
        pd = jnp.exp(logit_d - m)
        pp = jnp.exp(logit_p - m)
        m_scr[rows, :] = m
        l_scr[rows, :] = jnp.sum(pd, axis=-1, keepdims=True) + jnp.sum(pp, axis=-1, keepdims=True)
        acc_scr[rows, :] = (jnp.dot(pd.astype(BF16), vd, preferred_element_type=F32)
                            + jnp.dot(pp.astype(BF16), vp, preferred_element_type=F32))

    for r in range(sub):
        rows = slice(r * blk, (r + 1) * blk)
        bound_q = bounds[r]
        g = qi * sub + r

        def more_needed(j_done, rows=rows, bound_q=bound_q):
            first_c = ct_ref[0, j_done][0:1, 0:1]
            return (jnp.max(bound_q - first_c - m_scr[rows, :]) >= exit_below).astype(jnp.int32)

        def cond(state):
            j, go = state
            return jnp.logical_and(j >= 0, go > 0)

        def body(state, rows=rows, more_needed=more_needed):
            j, _ = state
            q = q_ref[rows, :]
            kb, vb = kv_block(j)
            c_q = jnp.sum(jnp.where(lane == h, c_ref[rows, :], 0.0), axis=-1, keepdims=True)
            logit = _qk(q, kb) + (c_q - ct_ref[0, j])
            m_old = m_scr[rows, :]
            m_new = jnp.maximum(m_old, jnp.max(logit, axis=-1, keepdims=True))
            alpha = jnp.exp(m_old - m_new)
            pr = jnp.exp(logit - m_new)
            l_scr[rows, :] = alpha * l_scr[rows, :] + jnp.sum(pr, axis=-1, keepdims=True)
            acc_scr[rows, :] = alpha * acc_scr[rows, :] + jnp.dot(pr.astype(BF16), vb,
                                                                  preferred_element_type=F32)
            m_scr[rows, :] = m_new
            return j - 1, more_needed(j)

        lax.while_loop(cond, body, (g - 2, more_needed(jnp.maximum(g - 1, 0))))

    o = acc_scr[...] / l_scr[...]
    y = _head_rmsnorm(o, gain_ref[0])
    gate = gate_ref[...].astype(F32)
    o_ref[...] = (y * (1.0 / (1.0 + jnp.exp(-gate)))).astype(BF16)


def _fox_attn(p, c, ct4, gain, batch, n_sb_heads, n_heads, exit_below):
    n = p.shape[0]
    s = n // batch
    blk, sub, nq = _attn_tiling(s)
    tq = blk * sub
    base = 3 * n_sb_heads
    kern = functools.partial(_fox_attn_kernel, blk=blk, sub=sub, exit_below=exit_below)
    return pl.pallas_call(
        kern,
        grid=(batch, n_heads, nq),
        in_specs=[
            pl.BlockSpec((tq, HEAD_DIM), lambda b, h, i: (b * nq + i, base + h)),
            pl.BlockSpec((s, HEAD_DIM), lambda b, h, i: (b, base + n_heads + h)),
            pl.BlockSpec((s, HEAD_DIM), lambda b, h, i: (b, base + 2 * n_heads + h)),
            pl.BlockSpec((tq, HEAD_DIM), lambda b, h, i: (b * nq + i, base + 3 * n_heads + h)),
            pl.BlockSpec((tq, LANES), lambda b, h, i: (b * nq + i, 0)),
            pl.BlockSpec((1, s // blk, 1, blk), lambda b, h, i: (b * n_heads + h, 0, 0, 0)),
            pl.BlockSpec((1, 1, HEAD_DIM), lambda b, h, i: (h, 0, 0)),
        ],
        out_specs=pl.BlockSpec((tq, HEAD_DIM), lambda b, h, i: (b * nq + i, h)),
        out_shape=jax.ShapeDtypeStruct((n, n_heads * HEAD_DIM), BF16),
        scratch_shapes=[
            pltpu.VMEM((tq, HEAD_DIM), F32),
            pltpu.VMEM((tq, 1), F32),
            pltpu.VMEM((tq, 1), F32),
            pltpu.VMEM((8, LANES), F32),
        ],
        compiler_params=_params("arbitrary", "arbitrary", "arbitrary"),
        name="fox_attn",
    )(p, p, p, p, c, ct4, gain)


def _out_proj_kernel(h_ref, a_ref, b_ref, wa_ref, wb_ref, o_ref):
    o_ref[...] = (h_ref[...]
                  + jnp.dot(a_ref[...], wa_ref[...], preferred_element_type=F32)
                  + jnp.dot(b_ref[...], wb_ref[...], preferred_element_type=F32))


def _out_proj(h, o_sb, o_fox, w_sb, w_fox):
    n, d = h.shape
    bm = min(ROW_BLOCK, n)
    return pl.pallas_call(
        _out_proj_kernel,
        grid=(n // bm,),
        in_specs=[
            pl.BlockSpec((bm, d), lambda i: (i, 0)),
            pl.BlockSpec((bm, o_sb.shape[1]), lambda i: (i, 0)),
            pl.BlockSpec((bm, o_fox.shape[1]), lambda i: (i, 0)),
            pl.BlockSpec(w_sb.shape, lambda i: (0, 0)),
            pl.BlockSpec(w_fox.shape, lambda i: (0, 0)),
        ],
        out_specs=pl.BlockSpec((bm, d), lambda i: (i, 0)),
        out_shape=jax.ShapeDtypeStruct((n, d), F32),
        compiler_params=_params("parallel"),
        name="out_proj",
    )(h, o_sb, o_fox, w_sb, w_fox)


def _router_gates(m32, m_hi, wr_hi_ref, wr_lo_ref, n_experts):
    m_lo = (m32 - m_hi.astype(F32)).astype(BF16)
    logits = (jnp.dot(m_hi, wr_hi_ref[...], preferred_element_type=F32)
              + jnp.dot(m_lo, wr_hi_ref[...], preferred_element_type=F32)
              + jnp.dot(m_hi, wr_lo_ref[...], preferred_element_type=F32))
    lane = lax.broadcasted_iota(jnp.int32, logits.shape, 1)
    valid = lane < n_experts
    logits = jnp.where(valid, logits, -jnp.inf)
    e = jnp.exp(logits - jnp.max(logits, axis=-1, keepdims=True))
    probs = e / jnp.sum(e, axis=-1, keepdims=True)
    p1 = jnp.max(probs, axis=-1, keepdims=True)
    i1 = jnp.min(jnp.where(probs == p1, lane, LANES), axis=-1, keepdims=True)
    rest = jnp.where(lane == i1, -1.0, probs)
    p2 = jnp.max(rest, axis=-1, keepdims=True)
    i2 = jnp.min(jnp.where(rest == p2, lane, LANES), axis=-1, keepdims=True)
    denom = p1 + p2
    first, second = lane == i1, lane == i2
    gates = jnp.where(first, p1 / denom, 0.0) + jnp.where(second, p2 / denom, 0.0)
    return gates, jnp.logical_or(first, second)


def _swiglu(m, wg, wu, wd):
    gate = jnp.dot(m, wg, preferred_element_type=F32)
    up = jnp.dot(m, wu, preferred_element_type=F32)
    act = ((gate * (1.0 / (1.0 + jnp.exp(-gate)))) * up).astype(BF16)
    return jnp.dot(act, wd, preferred_element_type=F32)


def _ffn_kernel(x_ref, g_ref, wg_ref, wu_ref, wd_ref, fin_ref, o_ref, m_scr, acc_scr, *, final_norm):
    j = pl.program_id(1)

    @pl.when(j == 0)
    def _():
        x = x_ref[...]
        m_scr[...] = ((x * _row_rms_scale(x)) * g_ref[...]).astype(BF16)
        acc_scr[...] = jnp.zeros_like(acc_scr)

    acc_scr[...] += _swiglu(m_scr[...], wg_ref[...], wu_ref[...], wd_ref[...])

    @pl.when(j == pl.num_programs(1) - 1)
    def _():
        out = x_ref[...] + acc_scr[...]
        if final_norm:
            out = (out * _row_rms_scale(out)) * fin_ref[...]
        o_ref[...] = out


def _ffn(h, g, wg, wu, wd, fin, final_norm):
    n, d = h.shape
    f = wg.shape[1]
    bm = min(ROW_BLOCK, n)
    bf = min(FF_BLOCK, f)
    return pl.pallas_call(
        functools.partial(_ffn_kernel, final_norm=final_norm),
        grid=(n // bm, f // bf),
        in_specs=[
            pl.BlockSpec((bm, d), lambda i, j: (i, 0)),
            pl.BlockSpec((1, d), lambda i, j: (0, 0)),
            pl.BlockSpec((d, bf), lambda i, j: (0, j)),
            pl.BlockSpec((d, bf), lambda i, j: (0, j)),
            pl.BlockSpec((bf, d), lambda i, j: (j, 0)),
            pl.BlockSpec((1, d), lambda i, j: (0, 0)),
        ],
        out_specs=pl.BlockSpec((bm, d), lambda i, j: (i, 0)),
        out_shape=jax.ShapeDtypeStruct((n, d), F32),
        scratch_shapes=[pltpu.VMEM((bm, d), BF16), pltpu.VMEM((bm, d), F32)],
        compiler_params=_params("parallel", "arbitrary"),
        name="dense_ffn",
    )(h, g, wg, wu, wd, fin)


def _route_kernel(x_ref, g_ref, wr_hi_ref, wr_lo_ref, gates_ref, sel_ref, *, n_experts):
    x = x_ref[...]
    m32 = (x * _row_rms_scale(x)) * g_ref[...]
    gates, chosen = _router_gates(m32, m32.astype(BF16), wr_hi_ref, wr_lo_ref, n_experts)
    gates_ref[...] = gates
    sel_ref[...] = jnp.where(chosen, 1.0, 0.0).astype(BF16)


def _route(h, g, wr_hi, wr_lo, n_experts):
    n, d = h.shape
    bm = min(ROW_BLOCK, n)
    return pl.pallas_call(
        functools.partial(_route_kernel, n_experts=n_experts),
        grid=(n // bm,),
        in_specs=[
            pl.BlockSpec((bm, d), lambda i: (i, 0)),
            pl.BlockSpec((1, d), lambda i: (0, 0)),
            pl.BlockSpec((d, LANES), lambda i: (0, 0)),
            pl.BlockSpec((d, LANES), lambda i: (0, 0)),
        ],
        out_specs=[pl.BlockSpec((bm, LANES), lambda i: (i, 0)),
                   pl.BlockSpec((bm, LANES), lambda i: (i, 0))],
        out_shape=[jax.ShapeDtypeStruct((n, LANES), F32),
                   jax.ShapeDtypeStruct((n, LANES), BF16)],
        compiler_params=_params("parallel"),
        name="moe_route",
    )(h, g, wr_hi, wr_lo)


PLAN_ROWS = 8


def _plan_kernel(sel_ref, info_ref, counts_ref, carry_scr):
    @pl.when(pl.program_id(0) == 0)
    def _():
        carry_scr[...] = jnp.zeros_like(carry_scr)

    sel = sel_ref[...]
    self32 = sel.astype(F32)
    bc = sel.shape[0]
    row = lax.broadcasted_iota(jnp.int32, (bc, bc), 0)
    col = lax.broadcasted_iota(jnp.int32, (bc, bc), 1)
    lower = jnp.where(col <= row, 1.0, 0.0).astype(BF16)
    cum = jnp.dot(lower, sel, preferred_element_type=F32) + carry_scr[...]
    rank = cum - self32
    lane = lax.broadcasted_iota(jnp.int32, sel.shape, 1)
    picked = self32 > 0.0
    e_lo = jnp.min(jnp.where(picked, lane, LANES), axis=-1, keepdims=True)
    e_hi = jnp.max(jnp.where(picked, lane, -1), axis=-1, keepdims=True)
    at_lo, at_hi = lane == e_lo, lane == e_hi

    def pick(mask, v):
        return jnp.sum(jnp.where(mask, v, 0.0), axis=-1, keepdims=True)

    cols = [e_lo.astype(F32), e_hi.astype(F32), pick(at_lo, rank), pick(at_hi, rank)]
    tile = jnp.zeros(sel.shape, F32)
    for idx, cval in enumerate(cols):
        tile = jnp.where(lane == idx, cval, tile)
    info_ref[...] = tile.T[:PLAN_ROWS, :]
    carry_scr[...] = cum[bc - 1:bc, :]
    counts_ref[...] = cum[bc - 1:bc, :]


def _plan(sel):
    n = sel.shape[0]
    bc = min(CUMSUM_BLOCK, n)
    return pl.pallas_call(
        _plan_kernel,
        grid=(n // bc,),
        in_specs=[pl.BlockSpec((bc, LANES), lambda t: (t, 0))],
        out_specs=[pl.BlockSpec((PLAN_ROWS, bc), lambda t: (0, t)),
                   pl.BlockSpec((1, LANES), lambda t: (0, 0))],
        out_shape=[jax.ShapeDtypeStruct((PLAN_ROWS, n), F32),
                   jax.ShapeDtypeStruct((1, LANES), F32)],
        scratch_shapes=[pltpu.VMEM((1, LANES), F32)],
        compiler_params=_params("arbitrary"),
        name="moe_plan",
    )(sel)


def _row_copy(src_ref, src_row, dst_ref, dst_row, sem):
    return pltpu.make_async_copy(src_ref.at[pl.ds(src_row, 1), :], dst_ref.at[pl.ds(dst_row, 1), :], sem)


def _dispatch_kernel(dst_ref, h_ref, xs_in_ref, xs_ref, sem, *, bt):
    del xs_in_ref
    base = pl.program_id(0) * bt

    def issue(t, carry):
        _row_copy(h_ref, base + t, xs_ref, dst_ref[0, t], sem).start()
        _row_copy(h_ref, base + t, xs_ref, dst_ref[1, t], sem).start()
        return carry

    lax.fori_loop(0, bt, issue, 0, unroll=ROW_COPY_UNROLL)
    for _ in range(N_EXPERTS_TOP_K):
        pltpu.make_async_copy(h_ref.at[pl.ds(0, bt), :], xs_ref.at[pl.ds(0, bt), :], sem).wait()


def _dispatch(h, dst, xs_zero):
    n, d = h.shape
    bt = min(DISPATCH_TOKENS, n)
    return pl.pallas_call(
        functools.partial(_dispatch_kernel, bt=bt),
        grid=(n // bt,),
        in_specs=[
            pl.BlockSpec((2, bt), lambda i: (0, i), memory_space=pltpu.SMEM),
            pl.BlockSpec(memory_space=pl.ANY),
            pl.BlockSpec(memory_space=pl.ANY),
        ],
        out_specs=pl.BlockSpec(memory_space=pl.ANY),
        out_shape=jax.ShapeDtypeStruct(xs_zero.shape, xs_zero.dtype),
        scratch_shapes=[pltpu.SemaphoreType.DMA],
        input_output_aliases={2: 0},
        compiler_params=_params("arbitrary"),
        name="moe_dispatch",
    )(dst, h, xs_zero)


def _experts_kernel(tile_expert_ref, n_active_ref, x_ref, g_ref, wg_ref, wu_ref, wd_ref, y_ref):
    del tile_expert_ref
    i = pl.program_id(0)

    @pl.when(i < n_active_ref[0])
    def _():
        x = x_ref[...]
        m = ((x * _row_rms_scale(x)) * g_ref[...]).astype(BF16)
        y_ref[...] = _swiglu(m, wg_ref[...], wu_ref[...], wd_ref[...])

    @pl.when(i >= n_active_ref[0])
    def _():
        y_ref[...] = jnp.zeros_like(y_ref)


def _experts(xs, g, wg, wu, wd, tile_expert, n_active, tile_rows):
    r, d = xs.shape
    _, _, f = wg.shape
    n_tiles = r // tile_rows
    return pl.pallas_call(
        _experts_kernel,
        grid_spec=pltpu.PrefetchScalarGridSpec(
            num_scalar_prefetch=2,
            grid=(n_tiles,),
            in_specs=[
                pl.BlockSpec((tile_rows, d), lambda i, te, na: (i, 0)),
                pl.BlockSpec((1, d), lambda i, te, na: (0, 0)),
                pl.BlockSpec((None, d, f), lambda i, te, na: (te[i], 0, 0)),
                pl.BlockSpec((None, d, f), lambda i, te, na: (te[i], 0, 0)),
                pl.BlockSpec((None, f, d), lambda i, te, na: (te[i], 0, 0)),
            ],
            out_specs=pl.BlockSpec((tile_rows, d), lambda i, te, na: (i, 0)),
        ),
        out_shape=jax.ShapeDtypeStruct((r, d), F32),
        compiler_params=_params("arbitrary"),
        name="moe_experts",
    )(tile_expert, n_active, xs, g, wg, wu, wd)


def _combine_kernel(dst_ref, h_ref, gates_ref, sel_ref, ys_ref, fin_ref, o_ref, y_lo, y_hi, sem, *,
                    bt, final_norm):
    def issue(t, carry):
        _row_copy(ys_ref, dst_ref[0, t], y_lo, t, sem).start()
        _row_copy(ys_ref, dst_ref[1, t], y_hi, t, sem).start(priority=1)
        return carry

    lax.fori_loop(0, bt, issue, 0, unroll=ROW_COPY_UNROLL)
    gates = gates_ref[...]
    picked = sel_ref[...].astype(F32) > 0.0
    lane = lax.broadcasted_iota(jnp.int32, gates.shape, 1)
    e_lo = jnp.min(jnp.where(picked, lane, LANES), axis=-1, keepdims=True)
    e_hi = jnp.max(jnp.where(picked, lane, -1), axis=-1, keepdims=True)
    g_lo = jnp.sum(jnp.where(lane == e_lo, gates, 0.0), axis=-1, keepdims=True)
    g_hi = jnp.sum(jnp.where(lane == e_hi, gates, 0.0), axis=-1, keepdims=True)
    pltpu.make_async_copy(ys_ref.at[pl.ds(0, bt), :], y_lo, sem).wait()
    pltpu.make_async_copy(ys_ref.at[pl.ds(0, bt), :], y_hi, sem).wait()
    out = h_ref[...] + g_lo * y_lo[...] + g_hi * y_hi[...]
    if final_norm:
        out = (out * _row_rms_scale(out)) * fin_ref[...]
    o_ref[...] = out


def _combine(h, dst, gates, sel, ys, fin, final_norm):
    n, d = h.shape
    bt = min(COMBINE_TOKENS, n)
    return pl.pallas_call(
        functools.partial(_combine_kernel, bt=bt, final_norm=final_norm),
        grid=(n // bt,),
        in_specs=[
            pl.BlockSpec((2, bt), lambda i: (0, i), memory_space=pltpu.SMEM),
            pl.BlockSpec((bt, d), lambda i: (i, 0)),
            pl.BlockSpec((bt, LANES), lambda i: (i, 0)),
            pl.BlockSpec((bt, LANES), lambda i: (i, 0)),
            pl.BlockSpec(memory_space=pl.ANY),
            pl.BlockSpec((1, d), lambda i: (0, 0)),
        ],
        out_specs=pl.BlockSpec((bt, d), lambda i: (i, 0)),
        out_shape=jax.ShapeDtypeStruct((n, d), F32),
        scratch_shapes=[pltpu.VMEM((bt, d), F32), pltpu.VMEM((bt, d), F32),
                        pltpu.SemaphoreType.DMA],
        compiler_params=_params("arbitrary"),
        name="moe_combine",
    )(dst, h, gates, sel, ys, fin)


def _moe(h, g, wg, wu, wd, wr_hi, wr_lo, fin, final_norm):
    n, d = h.shape
    n_experts = wg.shape[0]
    tile_rows = min(EXPERT_TILE_ROWS, n)
    gates, sel = _route(h, g, wr_hi, wr_lo, n_experts)
    info, counts = _plan(sel)
    counts = counts[0, :n_experts].astype(jnp.int32)
    padded = ((counts + tile_rows - 1) // tile_rows) * tile_rows
    ends = jnp.cumsum(padded)
    starts = ends - padded
    e_lo, e_hi = info[0].astype(jnp.int32), info[1].astype(jnp.int32)
    dst = jnp.stack([starts[e_lo] + info[2].astype(jnp.int32),
                     starts[e_hi] + info[3].astype(jnp.int32)])
    n_tiles = (N_EXPERTS_TOP_K * n) // tile_rows + n_experts
    tile_start = jnp.arange(n_tiles, dtype=jnp.int32) * tile_rows
    tile_expert = jnp.minimum(jnp.sum(tile_start[:, None] >= ends[None, :], axis=1),
                              n_experts - 1).astype(jnp.int32)
    n_active = (ends[-1] // tile_rows).reshape(1).astype(jnp.int32)
    xs = _dispatch(h, dst, jnp.zeros((n_tiles * tile_rows, d), F32))
    ys = _experts(xs, g, wg, wu, wd, tile_expert, n_active, tile_rows)
    return _combine(h, dst, gates, sel, ys, fin, final_norm)


def _pad_lanes(a):
    return jnp.pad(a, ((0, 0),) * (a.ndim - 1) + ((0, LANES - a.shape[-1]),))


def kernel(x, attn_norm, w_in, b_forget, fox_q_norm, fox_k_norm, sb_out_norm, fox_out_norm,
           w_out, ffn_norm, dense_w_gate, dense_w_up, dense_w_down, router_w, moe_w_gate,
           moe_w_up, moe_w_down, final_norm):
    batch, seq, d = x.shape
    depth = w_in.shape[0]
    n_fox = b_forget.shape[1]
    fox_width = n_fox * HEAD_DIM
    sb_width = sb_out_norm.shape[1]
    n_sb = sb_width // HEAD_DIM
    assert sb_width == fox_width, "projection column tiles assume equal head-group widths"
    assert w_in.shape[2] == 3 * sb_width + 4 * fox_width + n_fox
    n_main = 3 * sb_width + 4 * fox_width
    blk = min(ATTN_BLOCK, seq)
    assert seq % blk == 0 and n_fox <= 8 and depth >= 1
    scale = HEAD_DIM ** -0.5

    r = lax.broadcasted_iota(jnp.int32, (blk, blk), 0)
    c = lax.broadcasted_iota(jnp.int32, (blk, blk), 1)
    tri = (r >= c).astype(BF16)

    h = x.reshape(batch * seq, d)
    for layer in range(depth):
        w_main = w_in[layer, :, :n_main].astype(BF16)
        w_forget = _pad_lanes(w_in[layer, :, n_main:]).astype(BF16)
        qk_gain = jnp.stack([fox_q_norm[layer] * scale, fox_k_norm[layer]]).reshape(2, 1, HEAD_DIM)
        p, f = _in_proj(h, attn_norm[layer].reshape(1, d), w_main, w_forget, qk_gain, sb_width)
        c_tok, c_seq = _forget_cumsum(f, _pad_lanes(b_forget[layer].reshape(1, n_fox)), batch, n_fox)
        ct4 = c_seq.reshape(batch * n_fox, seq // blk, 1, blk)
        o_sb = _sb_attn(p, tri, sb_out_norm[layer].reshape(n_sb, 1, HEAD_DIM), batch, n_sb,
                        EXP_ZERO_BELOW)
        o_fox = _fox_attn(p, c_tok, ct4, fox_out_norm[layer].reshape(n_fox, 1, HEAD_DIM), batch,
                          n_sb, n_fox, EXP_ZERO_BELOW)
        w_o = w_out[layer].astype(BF16)
        h = _out_proj(h, o_sb, o_fox, w_o[:sb_width], w_o[sb_width:])
        i = layer // 2
        last = layer == depth - 1
        fin = final_norm.reshape(1, d)
        g_ffn = ffn_norm[layer].reshape(1, d)
        if layer % 2 == 0:
            h = _ffn(h, g_ffn, dense_w_gate[i].astype(BF16), dense_w_up[i].astype(BF16),
                     dense_w_down[i].astype(BF16), fin, last)
        else:
            wr = _pad_lanes(router_w[i])
            wr_hi = wr.astype(BF16)
            wr_lo = (wr - wr_hi.astype(F32)).astype(BF16)
            h = _moe(h, g_ffn, moe_w_gate[i].astype(BF16), moe_w_up[i].astype(BF16),
                     moe_w_down[i].astype(BF16), wr_hi, wr_lo, fin, last)
    return h.reshape(batch, seq, d)
```

```python
import functools

import jax
import jax.numpy as jnp
from jax import lax
from jax.experimental import pallas as pl
from jax.experimental.pallas import tpu as pltpu

F32 = jnp.float32
BF16 = jnp.bfloat16

HEAD_DIM = 128
LANES = 128
N_EXPERTS_TOP_K = 2
NORM_EPS = 1e-6
EXP_ZERO_BELOW = -104.0
NEG_BIG = -1e30
VMEM_LIMIT_BYTES = 56 * 1024 * 1024

ATTN_BLOCK = 256
ATTN_SUB_BLOCKS = 4
CUMSUM_BLOCK = 256
ROW_BLOCK = 512
IN_PROJ_ROW_BLOCK = 1024
FF_BLOCK = 512
EXPERT_TILE_ROWS = 512
DISPATCH_TOKENS = 512
COMBINE_TOKENS = 256
ROW_COPY_UNROLL = 8


def _params(*sem):
    return pltpu.CompilerParams(dimension_semantics=sem, vmem_limit_bytes=VMEM_LIMIT_BYTES)


def _row_rms_scale(x):
    return lax.rsqrt(jnp.mean(x * x, axis=-1, keepdims=True) + NORM_EPS)


def _in_proj_kernel(x_ref, g_ref, w_ref, wf_ref, qk_gain_ref, p_ref, f_ref, a_scr, *, scale):
    j = pl.program_id(1)

    @pl.when(j == 0)
    def _():
        x = x_ref[...]
        a = (x * _row_rms_scale(x)) * g_ref[...]
        a_scr[...] = a.astype(BF16)
        f_ref[...] = jnp.dot(a_scr[...], wf_ref[...], preferred_element_type=F32)

    acc = jnp.dot(a_scr[...], w_ref[...], preferred_element_type=F32)
    heads = acc.shape[1] // HEAD_DIM

    @pl.when(j == 0)
    def _():
        p_ref[...] = (acc * scale).astype(BF16)

    @pl.when(jnp.logical_or(j == 3, j == 4))
    def _():
        gain = qk_gain_ref[j - 3]
        for hh in range(heads):
            t = acc[:, hh * HEAD_DIM:(hh + 1) * HEAD_DIM]
            y = (t * _row_rms_scale(t)) * gain
            p_ref[:, hh * HEAD_DIM:(hh + 1) * HEAD_DIM] = y.astype(BF16)

    @pl.when(jnp.logical_and(j != 0, jnp.logical_and(j != 3, j != 4)))
    def _():
        p_ref[...] = acc.astype(BF16)


def _in_proj(h, g, w, wf, qk_gain, seg_width):
    n, d = h.shape
    bm = min(IN_PROJ_ROW_BLOCK, n)
    bn = seg_width
    n_tiles = w.shape[1] // bn
    kern = functools.partial(_in_proj_kernel, scale=HEAD_DIM ** -0.5)
    return pl.pallas_call(
        kern,
        grid=(n // bm, n_tiles),
        in_specs=[
            pl.BlockSpec((bm, d), lambda i, j: (i, 0)),
            pl.BlockSpec((1, d), lambda i, j: (0, 0)),
            pl.BlockSpec((d, bn), lambda i, j: (0, j)),
            pl.BlockSpec((d, LANES), lambda i, j: (0, 0)),
            pl.BlockSpec((2, 1, HEAD_DIM), lambda i, j: (0, 0, 0)),
        ],
        out_specs=[
            pl.BlockSpec((bm, bn), lambda i, j: (i, j)),
            pl.BlockSpec((bm, LANES), lambda i, j: (i, 0)),
        ],
        out_shape=[
            jax.ShapeDtypeStruct((n, w.shape[1]), BF16),
            jax.ShapeDtypeStruct((n, LANES), F32),
        ],
        scratch_shapes=[pltpu.VMEM((bm, d), BF16)],
        compiler_params=_params("parallel", "arbitrary"),
        name="in_proj",
    )(h, g, w, wf, qk_gain)


def _split3(x):
    hi = x.astype(BF16)
    r = x - hi.astype(F32)
    mid = r.astype(BF16)
    lo = (r - mid.astype(F32)).astype(BF16)
    return hi, mid, lo


def _forget_cumsum_kernel(f_ref, b_ref, c_ref, ct_ref, carry_scr, *, n_heads):
    @pl.when(pl.program_id(1) == 0)
    def _():
        carry_scr[...] = jnp.zeros_like(carry_scr)

    x = f_ref[...] + b_ref[...]
    log_f = jnp.minimum(x, 0.0) - jnp.log(1.0 + jnp.exp(-jnp.abs(x)))
    bc = x.shape[0]
    row = lax.broadcasted_iota(jnp.int32, (bc, bc), 0)
    col = lax.broadcasted_iota(jnp.int32, (bc, bc), 1)
    lower = jnp.where(col <= row, 1.0, 0.0).astype(BF16)
    hi, mid, lo = _split3(log_f)
    cum = (jnp.dot(lower, hi, preferred_element_type=F32)
           + jnp.dot(lower, mid, preferred_element_type=F32)
           + jnp.dot(lower, lo, preferred_element_type=F32))
    c = cum + carry_scr[...]
    c_ref[...] = c
    carry_scr[...] = c[bc - 1:bc, :]
    ct_ref[0] = c.T[:n_heads, :]


def _forget_cumsum(f, b_pad, batch, n_heads):
    n = f.shape[0]
    s = n // batch
    bc = min(CUMSUM_BLOCK, s)
    nblk = s // bc
    kern = functools.partial(_forget_cumsum_kernel, n_heads=n_heads)
    return pl.pallas_call(
        kern,
        grid=(batch, nblk),
        in_specs=[
            pl.BlockSpec((bc, LANES), lambda b, t: (b * nblk + t, 0)),
            pl.BlockSpec((1, LANES), lambda b, t: (0, 0)),
        ],
        out_specs=[
            pl.BlockSpec((bc, LANES), lambda b, t: (b * nblk + t, 0)),
            pl.BlockSpec((1, n_heads, bc), lambda b, t: (b, 0, t)),
        ],
        out_shape=[
            jax.ShapeDtypeStruct((n, LANES), F32),
            jax.ShapeDtypeStruct((batch, n_heads, s), F32),
        ],
        scratch_shapes=[pltpu.VMEM((1, LANES), F32)],
        compiler_params=_params("arbitrary", "arbitrary"),
        name="forget_cumsum",
    )(f, b_pad)


def _max_key_norm(k_ref, out_scr, chunk):
    n_chunks = k_ref.shape[0] // chunk

    def body(c, mx):
        kb = k_ref[pl.ds(pl.multiple_of(c * chunk, chunk), chunk), :].astype(F32)
        n2 = jnp.sum(kb * kb, axis=-1, keepdims=True)
        return jnp.maximum(mx, jnp.max(n2, axis=0, keepdims=True))

    mx = lax.fori_loop(0, n_chunks, body, jnp.zeros((1, 1), F32))
    out_scr[...] = jnp.broadcast_to(jnp.sqrt(mx), out_scr.shape)


def _qk(q, kb):
    return lax.dot_general(q, kb, (((1,), (1,)), ((), ())), preferred_element_type=F32)


def _head_rmsnorm(o, gain):
    return (o * _row_rms_scale(o)) * gain


def _softplus(z):
    return jnp.maximum(z, 0.0) + jnp.log(1.0 + jnp.exp(-jnp.abs(z)))


def _sb_attn_kernel(q_ref, k_ref, v_ref, tri_ref, gain_ref, o_ref,
                    acc_scr, carry_scr, kmax_scr, *, blk, sub, exit_below):
    qi = pl.program_id(2)

    @pl.when(qi == 0)
    def _():
        _max_key_norm(k_ref, kmax_scr, blk)

    tri = tri_ref[...]
    kmax = kmax_scr[0:1, 0:1]
    r_idx = lax.broadcasted_iota(jnp.int32, (blk, blk), 0)
    c_idx = lax.broadcasted_iota(jnp.int32, (blk, blk), 1)
    causal = c_idx < r_idx

    def kv_block(j):
        start = pl.multiple_of(j * blk, blk)
        return k_ref[pl.ds(start, blk), :], v_ref[pl.ds(start, blk), :]

    z_bounds, kvs, zs, sps, withins = [], [], [], [], []
    for r in range(sub):
        g = qi * sub + r
        q = q_ref[r * blk:(r + 1) * blk, :]
        qf = q.astype(F32)
        q_norm = jnp.sqrt(jnp.sum(qf * qf, axis=-1, keepdims=True))
        z_bounds.append(q_norm * kmax * 1.001 + 1e-3)
        kd, vd = kv_block(g)
        kp, vp = kv_block(jnp.maximum(g - 1, 0))
        kvs.append((vd, vp))
        zs.append((_qk(q, kd), _qk(q, kp)))
    for r in range(sub):
        z_d, z_p = zs[r]
        sps.append((jnp.where(causal, _softplus(z_d), 0.0), _softplus(z_p)))
    for r in range(sub):
        sp_d, sp_p = sps[r]
        withins.append((jnp.dot(sp_d.astype(BF16), tri, preferred_element_type=F32),
                        jnp.dot(sp_p.astype(BF16), tri, preferred_element_type=F32)))
    for r in range(sub):
        rows = slice(r * blk, (r + 1) * blk)
        (z_d, z_p), (sp_d, sp_p), (in_d, in_p), (vd, vp) = zs[r], sps[r], withins[r], kvs[r]
        carry = jnp.sum(sp_d, axis=-1, keepdims=True)
        w_d = jnp.where(causal, jnp.exp(z_d - in_d), 0.0)
        rev_base = carry if r > 0 else carry + jnp.where(qi > 0, 0.0, -NEG_BIG)
        w_p = jnp.exp(z_p - (in_p + rev_base))
        acc_scr[rows, :] = (jnp.dot(w_d.astype(BF16), vd, preferred_element_type=F32)
                            + jnp.dot(w_p.astype(BF16), vp, preferred_element_type=F32))
        carry_scr[rows, :] = carry + jnp.sum(sp_p, axis=-1, keepdims=True)

    def more_needed(it):
        worst = jnp.full((1, 1), NEG_BIG, F32)
        for r in range(sub):
            rows = slice(r * blk, (r + 1) * blk)
            slack = jnp.max(z_bounds[r] - carry_scr[rows, :], axis=0, keepdims=True)
            worst = jnp.maximum(worst, jnp.where(qi * sub + r - 2 - it >= 0, slack, NEG_BIG))
        return (jnp.max(worst) >= exit_below).astype(jnp.int32)

    def cond(state):
        _, go = state
        return go > 0

    def body(state):
        it, _ = state
        for r in range(sub):
            rows = slice(r * blk, (r + 1) * blk)
            j = qi * sub + r - 2 - it
            kb, vb = kv_block(jnp.maximum(j, 0))
            z = _qk(q_ref[rows, :], kb)
            sp = _softplus(z)
            within = jnp.dot(sp.astype(BF16), tri, preferred_element_type=F32)
            rev_base = carry_scr[rows, :] + jnp.where(j >= 0, 0.0, -NEG_BIG)
            w = jnp.exp(z - (within + rev_base))
            acc_scr[rows, :] += jnp.dot(w.astype(BF16), vb, preferred_element_type=F32)
            carry_scr[rows, :] += jnp.sum(sp, axis=-1, keepdims=True)
        return it + 1, more_needed(it + 1)

    lax.while_loop(cond, body, (0, more_needed(0)))
    o_ref[...] = _head_rmsnorm(acc_scr[...], gain_ref[0]).astype(BF16)


def _attn_tiling(s):
    blk = min(ATTN_BLOCK, s)
    sub = min(ATTN_SUB_BLOCKS, s // blk)
    return blk, sub, s // (blk * sub)


def _sb_attn(p, tri, gain, batch, n_heads, exit_below):
    n = p.shape[0]
    s = n // batch
    blk, sub, nq = _attn_tiling(s)
    tq = blk * sub
    kern = functools.partial(_sb_attn_kernel, blk=blk, sub=sub, exit_below=exit_below)
    return pl.pallas_call(
        kern,
        grid=(batch, n_heads, nq),
        in_specs=[
            pl.BlockSpec((tq, HEAD_DIM), lambda b, h, i: (b * nq + i, h)),
            pl.BlockSpec((s, HEAD_DIM), lambda b, h, i: (b, n_heads + h)),
            pl.BlockSpec((s, HEAD_DIM), lambda b, h, i: (b, 2 * n_heads + h)),
            pl.BlockSpec((blk, blk), lambda b, h, i: (0, 0)),
            pl.BlockSpec((1, 1, HEAD_DIM), lambda b, h, i: (h, 0, 0)),
        ],
        out_specs=pl.BlockSpec((tq, HEAD_DIM), lambda b, h, i: (b * nq + i, h)),
        out_shape=jax.ShapeDtypeStruct((n, n_heads * HEAD_DIM), BF16),
        scratch_shapes=[
            pltpu.VMEM((tq, HEAD_DIM), F32),
            pltpu.VMEM((tq, 1), F32),
            pltpu.VMEM((8, LANES), F32),
        ],
        compiler_params=_params("arbitrary", "arbitrary", "arbitrary"),
        name="sb_attn",
    )(p, p, p, tri, gain)


def _fox_attn_kernel(q_ref, k_ref, v_ref, gate_ref, c_ref, ct_ref, gain_ref, o_ref,
                     acc_scr, m_scr, l_scr, kmax_scr, *, blk, sub, exit_below):
    h = pl.program_id(1)
    qi = pl.program_id(2)

    @pl.when(qi == 0)
    def _():
        _max_key_norm(k_ref, kmax_scr, blk)

    kmax = kmax_scr[0:1, 0:1]
    r_idx = lax.broadcasted_iota(jnp.int32, (blk, blk), 0)
    c_idx = lax.broadcasted_iota(jnp.int32, (blk, blk), 1)
    causal = c_idx <= r_idx
    lane = lax.broadcasted_iota(jnp.int32, (blk, LANES), 1)

    def kv_block(j):
        start = pl.multiple_of(j * blk, blk)
        return k_ref[pl.ds(start, blk), :], v_ref[pl.ds(start, blk), :]

    bounds = []
    for r in range(sub):
        rows = slice(r * blk, (r + 1) * blk)
        g = qi * sub + r
        gp = jnp.maximum(g - 1, 0)
        q = q_ref[rows, :]
        qf = q.astype(F32)
        q_norm = jnp.sqrt(jnp.sum(qf * qf, axis=-1, keepdims=True))
        c_q = jnp.sum(jnp.where(lane == h, c_ref[rows, :], 0.0), axis=-1, keepdims=True)
        bounds.append((q_norm * kmax * 1.001 + 1.0) + c_q)
        kd, vd = kv_block(g)
        kp, vp = kv_block(gp)
        c_kd = ct_ref[0, g]
        c_kp = ct_ref[0, gp]
        logit_d = jnp.where(causal, _qk(q, kd) + (c_q - c_kd), -jnp.inf)
        c_q_prev = c_q if r > 0 else c_q + jnp.where(g > 0, 0.0, NEG_BIG)
        logit_p = _qk(q, kp) + (c_q_prev - c_kp)
        m = jnp.maximum(jnp.max(logit_d, axis=-1, keepdims=True),
                        jnp.max(logit_p, axis=-1, keepdims=True))
        pd = jnp.exp(logit_d - m)
        pp = jnp.exp(logit_p - m)
        m_scr[rows, :] = m
        l_scr[rows, :] = jnp.sum(pd, axis=-1, keepdims=True) + jnp.sum(pp, axis=-1, keepdims=True)
        acc_scr[rows, :] = (jnp.dot(pd.astype(BF16), vd, preferred_element_type=F32)
                            + jnp.dot(pp.astype(BF16), vp, preferred_element_type=F32))

    def more_needed(it):
        worst = jnp.full((1, 1), NEG_BIG, F32)
        for r in range(sub):
            rows = slice(r * blk, (r + 1) * blk)
            j = qi * sub + r - 2 - it
            first_c = ct_ref[0, jnp.maximum(j + 1, 0)][0:1, 0:1]
            slack = jnp.max(bounds[r] - first_c - m_scr[rows, :], axis=0, keepdims=True)
            worst = jnp.maximum(worst, jnp.where(j >= 0, slack, NEG_BIG))
        return (jnp.max(worst) >= exit_below).astype(jnp.int32)

    def cond(state):
        _, go = state
        return go > 0

    def body(state):
        it, _ = state
        for r in range(sub):
            rows = slice(r * blk, (r + 1) * blk)
            j = qi * sub + r - 2 - it
            jc = jnp.maximum(j, 0)
            kb, vb = kv_block(jc)
            c_q = jnp.sum(jnp.where(lane == h, c_ref[rows, :], 0.0), axis=-1, keepdims=True)
            c_q = c_q + jnp.where(j >= 0, 0.0, NEG_BIG)
            logit = _qk(q_ref[rows, :], kb) + (c_q - ct_ref[0, jc])
            m_old = m_scr[rows, :]
            m_new = jnp.maximum(m_old, jnp.max(logit, axis=-1, keepdims=True))
            alpha = jnp.exp(m_old - m_new)
            pr = jnp.exp(logit - m_new)
            l_scr[rows, :] = alpha * l_scr[rows, :] + jnp.sum(pr, axis=-1, keepdims=True)
            acc_scr[rows, :] = alpha * acc_scr[rows, :] + jnp.dot(pr.astype(BF16), vb,
                                                                  preferred_element_type=F32)
            m_scr[rows, :] = m_new
        return it + 1, more_needed(it + 1)

    lax.while_loop(cond, body, (0, more_needed(0)))
    o = acc_scr[...] / l_scr[...]
    y = _head_rmsnorm(o, gain_ref[0])
    gate = gate_ref[...].astype(F32)
    o_ref[...] = (y * (1.0 / (1.0 + jnp.exp(-gate)))).astype(BF16)


def _fox_attn(p, c, ct4, gain, batch, n_sb_heads, n_heads, exit_below):
    n = p.shape[0]
    s = n // batch
    blk, sub, nq = _attn_tiling(s)
    tq = blk * sub
    base = 3 * n_sb_heads
    kern = functools.partial(_fox_attn_kernel, blk=blk, sub=sub, exit_below=exit_below)
    return pl.pallas_call(
        kern,
        grid=(batch, n_heads, nq),
        in_specs=[
            pl.BlockSpec((tq, HEAD_DIM), lambda b, h, i: (b * nq + i, base + h)),
            pl.BlockSpec((s, HEAD_DIM), lambda b, h, i: (b, base + n_heads + h)),
            pl.BlockSpec((s, HEAD_DIM), lambda b, h, i: (b, base + 2 * n_heads + h)),
            pl.BlockSpec((tq, HEAD_DIM), lambda b, h, i: (b * nq + i, base + 3 * n_heads + h)),
            pl.BlockSpec((tq, LANES), lambda b, h, i: (b * nq + i, 0)),
            pl.BlockSpec((1, s // blk, 1, blk), lambda b, h, i: (b * n_heads + h, 0, 0, 0)),
            pl.BlockSpec((1, 1, HEAD_DIM), lambda b, h, i: (h, 0, 0)),
        ],
        out_specs=pl.BlockSpec((tq, HEAD_DIM), lambda b, h, i: (b * nq + i, h)),
        out_shape=jax.ShapeDtypeStruct((n, n_heads * HEAD_DIM), BF16),
        scratch_shapes=[
            pltpu.VMEM((tq, HEAD_DIM), F32),
            pltpu.VMEM((tq, 1), F32),
            pltpu.VMEM((tq, 1), F32),
            pltpu.VMEM((8, LANES), F32),
        ],
        compiler_params=_params("arbitrary", "arbitrary", "arbitrary"),
        name="fox_attn",
    )(p, p, p, p, c, ct4, gain)


def _out_proj_kernel(h_ref, a_ref, b_ref, wa_ref, wb_ref, o_ref):
    o_ref[...] = (h_ref[...]
                  + jnp.dot(a_ref[...], wa_ref[...], preferred_element_type=F32)
                  + jnp.dot(b_ref[...], wb_ref[...], preferred_element_type=F32))


def _out_proj(h, o_sb, o_fox, w_sb, w_fox):
    n, d = h.shape
    bm = min(ROW_BLOCK, n)
    return pl.pallas_call(
        _out_proj_kernel,
        grid=(n // bm,),
        in_specs=[
            pl.BlockSpec((bm, d), lambda i: (i, 0)),
            pl.BlockSpec((bm, o_sb.shape[1]), lambda i: (i, 0)),
            pl.BlockSpec((bm, o_fox.shape[1]), lambda i: (i, 0)),
            pl.BlockSpec(w_sb.shape, lambda i: (0, 0)),
            pl.BlockSpec(w_fox.shape, lambda i: (0, 0)),
        ],
        out_specs=pl.BlockSpec((bm, d), lambda i: (i, 0)),
        out_shape=jax.ShapeDtypeStruct((n, d), F32),
        compiler_params=_params("parallel"),
        name="out_proj",
    )(h, o_sb, o_fox, w_sb, w_fox)


def _router_gates(m32, m_hi, wr_hi_ref, wr_lo_ref, n_experts):
    m_lo = (m32 - m_hi.astype(F32)).astype(BF16)
    logits = (jnp.dot(m_hi, wr_hi_ref[...], preferred_element_type=F32)
              + jnp.dot(m_lo, wr_hi_ref[...], preferred_element_type=F32)
              + jnp.dot(m_hi, wr_lo_ref[...], preferred_element_type=F32))
    lane = lax.broadcasted_iota(jnp.int32, logits.shape, 1)
    valid = lane < n_experts
    logits = jnp.where(valid, logits, -jnp.inf)
    e = jnp.exp(logits - jnp.max(logits, axis=-1, keepdims=True))
    probs = e / jnp.sum(e, axis=-1, keepdims=True)
    p1 = jnp.max(probs, axis=-1, keepdims=True)
    i1 = jnp.min(jnp.where(probs == p1, lane, LANES), axis=-1, keepdims=True)
    rest = jnp.where(lane == i1, -1.0, probs)
    p2 = jnp.max(rest, axis=-1, keepdims=True)
    i2 = jnp.min(jnp.where(rest == p2, lane, LANES), axis=-1, keepdims=True)
    denom = p1 + p2
    first, second = lane == i1, lane == i2
    gates = jnp.where(first, p1 / denom, 0.0) + jnp.where(second, p2 / denom, 0.0)
    return gates, jnp.logical_or(first, second)


def _swiglu(m, wg, wu, wd):
    gate = jnp.dot(m, wg, preferred_element_type=F32)
    up = jnp.dot(m, wu, preferred_element_type=F32)
    act = ((gate * (1.0 / (1.0 + jnp.exp(-gate)))) * up).astype(BF16)
    return jnp.dot(act, wd, preferred_element_type=F32)


def _ffn_kernel(x_ref, g_ref, wg_ref, wu_ref, wd_ref, fin_ref, o_ref, m_scr, acc_scr, *, final_norm):
    j = pl.program_id(1)

    @pl.when(j == 0)
    def _():
        x = x_ref[...]
        m_scr[...] = ((x * _row_rms_scale(x)) * g_ref[...]).astype(BF16)
        acc_scr[...] = jnp.zeros_like(acc_scr)

    acc_scr[...] += _swiglu(m_scr[...], wg_ref[...], wu_ref[...], wd_ref[...])

    @pl.when(j == pl.num_programs(1) - 1)
    def _():
        out = x_ref[...] + acc_scr[...]
        if final_norm:
            out = (out * _row_rms_scale(out)) * fin_ref[...]
        o_ref[...] = out


def _ffn(h, g, wg, wu, wd, fin, final_norm):
    n, d = h.shape
    f = wg.shape[1]
    bm = min(ROW_BLOCK, n)
    bf = min(FF_BLOCK, f)
    return pl.pallas_call(
        functools.partial(_ffn_kernel, final_norm=final_norm),
        grid=(n // bm, f // bf),
        in_specs=[
            pl.BlockSpec((bm, d), lambda i, j: (i, 0)),
            pl.BlockSpec((1, d), lambda i, j: (0, 0)),
            pl.BlockSpec((d, bf), lambda i, j: (0, j)),
            pl.BlockSpec((d, bf), lambda i, j: (0, j)),
            pl.BlockSpec((bf, d), lambda i, j: (j, 0)),
            pl.BlockSpec((1, d), lambda i, j: (0, 0)),
        ],
        out_specs=pl.BlockSpec((bm, d), lambda i, j: (i, 0)),
        out_shape=jax.ShapeDtypeStruct((n, d), F32),
        scratch_shapes=[pltpu.VMEM((bm, d), BF16), pltpu.VMEM((bm, d), F32)],
        compiler_params=_params("parallel", "arbitrary"),
        name="dense_ffn",
    )(h, g, wg, wu, wd, fin)


def _route_kernel(x_ref, g_ref, wr_hi_ref, wr_lo_ref, gates_ref, sel_ref, *, n_experts):
    x = x_ref[...]
    m32 = (x * _row_rms_scale(x)) * g_ref[...]
    gates, chosen = _router_gates(m32, m32.astype(BF16), wr_hi_ref, wr_lo_ref, n_experts)
    gates_ref[...] = gates
    sel_ref[...] = jnp.where(chosen, 1.0, 0.0).astype(BF16)


def _route(h, g, wr_hi, wr_lo, n_experts):
    n, d = h.shape
    bm = min(ROW_BLOCK, n)
    return pl.pallas_call(
        functools.partial(_route_kernel, n_experts=n_experts),
        grid=(n // bm,),
        in_specs=[
            pl.BlockSpec((bm, d), lambda i: (i, 0)),
            pl.BlockSpec((1, d), lambda i: (0, 0)),
            pl.BlockSpec((d, LANES), lambda i: (0, 0)),
            pl.BlockSpec((d, LANES), lambda i: (0, 0)),
        ],
        out_specs=[pl.BlockSpec((bm, LANES), lambda i: (i, 0)),
                   pl.BlockSpec((bm, LANES), lambda i: (i, 0))],
        out_shape=[jax.ShapeDtypeStruct((n, LANES), F32),
                   jax.ShapeDtypeStruct((n, LANES), BF16)],
        compiler_params=_params("parallel"),
        name="moe_route",
    )(h, g, wr_hi, wr_lo)


PLAN_ROWS = 8


def _plan_kernel(sel_ref, info_ref, counts_ref, carry_scr):
    @pl.when(pl.program_id(0) == 0)
    def _():
        carry_scr[...] = jnp.zeros_like(carry_scr)

    sel = sel_ref[...]
    self32 = sel.astype(F32)
    bc = sel.shape[0]
    row = lax.broadcasted_iota(jnp.int32, (bc, bc), 0)
    col = lax.broadcasted_iota(jnp.int32, (bc, bc), 1)
    lower = jnp.where(col <= row, 1.0, 0.0).astype(BF16)
    cum = jnp.dot(lower, sel, preferred_element_type=F32) + carry_scr[...]
    rank = cum - self32
    lane = lax.broadcasted_iota(jnp.int32, sel.shape, 1)
    picked = self32 > 0.0
    e_lo = jnp.min(jnp.where(picked, lane, LANES), axis=-1, keepdims=True)
    e_hi = jnp.max(jnp.where(picked, lane, -1), axis=-1, keepdims=True)
    at_lo, at_hi = lane == e_lo, lane == e_hi

    def pick(mask, v):
        return jnp.sum(jnp.where(mask, v, 0.0), axis=-1, keepdims=True)

    cols = [e_lo.astype(F32), e_hi.astype(F32), pick(at_lo, rank), pick(at_hi, rank)]
    tile = jnp.zeros(sel.shape, F32)
    for idx, cval in enumerate(cols):
        tile = jnp.where(lane == idx, cval, tile)
    info_ref[...] = tile.T[:PLAN_ROWS, :]
    carry_scr[...] = cum[bc - 1:bc, :]
    counts_ref[...] = cum[bc - 1:bc, :]


def _plan(sel):
    n = sel.shape[0]
    bc = min(CUMSUM_BLOCK, n)
    return pl.pallas_call(
        _plan_kernel,
        grid=(n // bc,),
        in_specs=[pl.BlockSpec((bc, LANES), lambda t: (t, 0))],
        out_specs=[pl.BlockSpec((PLAN_ROWS, bc), lambda t: (0, t)),
                   pl.BlockSpec((1, LANES), lambda t: (0, 0))],
        out_shape=[jax.ShapeDtypeStruct((PLAN_ROWS, n), F32),
                   jax.ShapeDtypeStruct((1, LANES), F32)],
        scratch_shapes=[pltpu.VMEM((1, LANES), F32)],
        compiler_params=_params("arbitrary"),
        name="moe_plan",
    )(sel)


def _row_copy(src_ref, src_row, dst_ref, dst_row, sem):
    return pltpu.make_async_copy(src_ref.at[pl.ds(src_row, 1), :], dst_ref.at[pl.ds(dst_row, 1), :], sem)


def _dispatch_kernel(dst_ref, h_ref, xs_in_ref, xs_ref, sem, *, bt):
    del xs_in_ref

    def issue(t, carry):
        _row_copy(h_ref, t, xs_ref, dst_ref[0, t], sem).start()
        _row_copy(h_ref, t, xs_ref, dst_ref[1, t], sem).start(priority=1)
        return carry

    lax.fori_loop(0, bt, issue, 0, unroll=ROW_COPY_UNROLL)
    for _ in range(N_EXPERTS_TOP_K):
        pltpu.make_async_copy(h_ref, xs_ref.at[pl.ds(0, bt), :], sem).wait()


def _dispatch(h, dst, xs_zero):
    n, d = h.shape
    bt = min(DISPATCH_TOKENS, n)
    return pl.pallas_call(
        functools.partial(_dispatch_kernel, bt=bt),
        grid=(n // bt,),
        in_specs=[
            pl.BlockSpec((2, bt), lambda i: (0, i), memory_space=pltpu.SMEM),
            pl.BlockSpec((bt, d), lambda i: (i, 0)),
            pl.BlockSpec(memory_space=pl.ANY),
        ],
        out_specs=pl.BlockSpec(memory_space=pl.ANY),
        out_shape=jax.ShapeDtypeStruct(xs_zero.shape, xs_zero.dtype),
        scratch_shapes=[pltpu.SemaphoreType.DMA],
        input_output_aliases={2: 0},
        compiler_params=_params("arbitrary"),
        name="moe_dispatch",
    )(dst, h, xs_zero)


def _experts_kernel(tile_expert_ref, n_active_ref, x_ref, g_ref, wg_ref, wu_ref, wd_ref, y_ref):
    del tile_expert_ref
    i = pl.program_id(0)

    @pl.when(i < n_active_ref[0])
    def _():
        x = x_ref[...]
        m = ((x * _row_rms_scale(x)) * g_ref[...]).astype(BF16)
        y_ref[...] = _swiglu(m, wg_ref[...], wu_ref[...], wd_ref[...])

    @pl.when(i >= n_active_ref[0])
    def _():
        y_ref[...] = jnp.zeros_like(y_ref)


def _experts(xs, g, wg, wu, wd, tile_expert, n_active, tile_rows):
    r, d = xs.shape
    _, _, f = wg.shape
    n_tiles = r // tile_rows
    return pl.pallas_call(
        _experts_kernel,
        grid_spec=pltpu.PrefetchScalarGridSpec(
            num_scalar_prefetch=2,
            grid=(n_tiles,),
            in_specs=[
                pl.BlockSpec((tile_rows, d), lambda i, te, na: (i, 0)),
                pl.BlockSpec((1, d), lambda i, te, na: (0, 0)),
                pl.BlockSpec((None, d, f), lambda i, te, na: (te[i], 0, 0)),
                pl.BlockSpec((None, d, f), lambda i, te, na: (te[i], 0, 0)),
                pl.BlockSpec((None, f, d), lambda i, te, na: (te[i], 0, 0)),
            ],
            out_specs=pl.BlockSpec((tile_rows, d), lambda i, te, na: (i, 0)),
        ),
        out_shape=jax.ShapeDtypeStruct((r, d), F32),
        compiler_params=_params("arbitrary"),
        name="moe_experts",
    )(tile_expert, n_active, xs, g, wg, wu, wd)


def _combine_kernel(dst_ref, h_ref, gates_ref, sel_ref, ys_ref, fin_ref, o_ref, y_lo, y_hi, sem, *,
                    bt, final_norm):
    def issue(t, carry):
        _row_copy(ys_ref, dst_ref[0, t], y_lo, t, sem).start()
        _row_copy(ys_ref, dst_ref[1, t], y_hi, t, sem).start(priority=1)
        return carry

    lax.fori_loop(0, bt, issue, 0, unroll=ROW_COPY_UNROLL)
    gates = gates_ref[...]
    picked = sel_ref[...].astype(F32) > 0.0
    lane = lax.broadcasted_iota(jnp.int32, gates.shape, 1)
    e_lo = jnp.min(jnp.where(picked, lane, LANES), axis=-1, keepdims=True)
    e_hi = jnp.max(jnp.where(picked, lane, -1), axis=-1, keepdims=True)
    g_lo = jnp.sum(jnp.where(lane == e_lo, gates, 0.0), axis=-1, keepdims=True)
    g_hi = jnp.sum(jnp.where(lane == e_hi, gates, 0.0), axis=-1, keepdims=True)
    pltpu.make_async_copy(ys_ref.at[pl.ds(0, bt), :], y_lo, sem).wait()
    pltpu.make_async_copy(ys_ref.at[pl.ds(0, bt), :], y_hi, sem).wait()
    out = h_ref[...] + g_lo * y_lo[...] + g_hi * y_hi[...]
    if final_norm:
        out = (out * _row_rms_scale(out)) * fin_ref[...]
    o_ref[...] = out


def _combine(h, dst, gates, sel, ys, fin, final_norm):
    n, d = h.shape
    bt = min(COMBINE_TOKENS, n)
    return pl.pallas_call(
        functools.partial(_combine_kernel, bt=bt, final_norm=final_norm),
        grid=(n // bt,),
        in_specs=[
            pl.BlockSpec((2, bt), lambda i: (0, i), memory_space=pltpu.SMEM),
            pl.BlockSpec((bt, d), lambda i: (i, 0)),
            pl.BlockSpec((bt, LANES), lambda i: (i, 0)),
            pl.BlockSpec((bt, LANES), lambda i: (i, 0)),
            pl.BlockSpec(memory_space=pl.ANY),
            pl.BlockSpec((1, d), lambda i: (0, 0)),
        ],
        out_specs=pl.BlockSpec((bt, d), lambda i: (i, 0)),
        out_shape=jax.ShapeDtypeStruct((n, d), F32),
        scratch_shapes=[pltpu.VMEM((bt, d), F32), pltpu.VMEM((bt, d), F32),
                        pltpu.SemaphoreType.DMA],
        compiler_params=_params("arbitrary"),
        name="moe_combine",
    )(dst, h, gates, sel, ys, fin)


def _moe(h, g, wg, wu, wd, wr_hi, wr_lo, fin, final_norm):
    n, d = h.shape
    n_experts = wg.shape[0]
    tile_rows = min(EXPERT_TILE_ROWS, n)
    gates, sel = _route(h, g, wr_hi, wr_lo, n_experts)
    info, counts = _plan(sel)
    counts = counts[0, :n_experts].astype(jnp.int32)
    padded = ((counts + tile_rows - 1) // tile_rows) * tile_rows
    ends = jnp.cumsum(padded)
    starts = ends - padded
    e_lo, e_hi = info[0].astype(jnp.int32), info[1].astype(jnp.int32)
    dst = jnp.stack([starts[e_lo] + info[2].astype(jnp.int32),
                     starts[e_hi] + info[3].astype(jnp.int32)])
    n_tiles = (N_EXPERTS_TOP_K * n) // tile_rows + n_experts
    tile_start = jnp.arange(n_tiles, dtype=jnp.int32) * tile_rows
    tile_expert = jnp.minimum(jnp.sum(tile_start[:, None] >= ends[None, :], axis=1),
                              n_experts - 1).astype(jnp.int32)
    n_active = (ends[-1] // tile_rows).reshape(1).astype(jnp.int32)
    xs = _dispatch(h, dst, jnp.zeros((n_tiles * tile_rows, d), F32))
    ys = _experts(xs, g, wg, wu, wd, tile_expert, n_active, tile_rows)
    return _combine(h, dst, gates, sel, ys, fin, final_norm)


def _pad_lanes(a):
    return jnp.pad(a, ((0, 0),) * (a.ndim - 1) + ((0, LANES - a.shape[-1]),))


def kernel(x, attn_norm, w_in, b_forget, fox_q_norm, fox_k_norm, sb_out_norm, fox_out_norm,
           w_out, ffn_norm, dense_w_gate, dense_w_up, dense_w_down, router_w, moe_w_gate,
           moe_w_up, moe_w_down, final_norm):
    batch, seq, d = x.shape
    depth = w_in.shape[0]
    n_fox = b_forget.shape[1]
    fox_width = n_fox * HEAD_DIM
    sb_width = sb_out_norm.shape[1]
    n_sb = sb_width // HEAD_DIM
    assert sb_width == fox_width, "projection column tiles assume equal head-group widths"
    assert w_in.shape[2] == 3 * sb_width + 4 * fox_width + n_fox
    n_main = 3 * sb_width + 4 * fox_width
    blk = min(ATTN_BLOCK, seq)
    assert seq % blk == 0 and n_fox <= 8 and depth >= 1
    scale = HEAD_DIM ** -0.5

    r = lax.broadcasted_iota(jnp.int32, (blk, blk), 0)
    c = lax.broadcasted_iota(jnp.int32, (blk, blk), 1)
    tri = (r >= c).astype(BF16)

    h = x.reshape(batch * seq, d)
    for layer in range(depth):
        w_main = w_in[layer, :, :n_main].astype(BF16)
        w_forget = _pad_lanes(w_in[layer, :, n_main:]).astype(BF16)
        qk_gain = jnp.stack([fox_q_norm[layer] * scale, fox_k_norm[layer]]).reshape(2, 1, HEAD_DIM)
        p, f = _in_proj(h, attn_norm[layer].reshape(1, d), w_main, w_forget, qk_gain, sb_width)
        c_tok, c_seq = _forget_cumsum(f, _pad_lanes(b_forget[layer].reshape(1, n_fox)), batch, n_fox)
        ct4 = c_seq.reshape(batch * n_fox, seq // blk, 1, blk)
        o_sb = _sb_attn(p, tri, sb_out_norm[layer].reshape(n_sb, 1, HEAD_DIM), batch, n_sb,
                        EXP_ZERO_BELOW)
        o_fox = _fox_attn(p, c_tok, ct4, fox_out_norm[layer].reshape(n_fox, 1, HEAD_DIM), batch,
                          n_sb, n_fox, EXP_ZERO_BELOW)
        w_o = w_out[layer].astype(BF16)
        h = _out_proj(h, o_sb, o_fox, w_o[:sb_width], w_o[sb_width:])
        i = layer // 2
        last = layer == depth - 1
        fin = final_norm.reshape(1, d)
        g_ffn = ffn_norm[layer].reshape(1, d)
        if layer % 2 == 0:
            h = _ffn(h, g_ffn, dense_w_gate[i].astype(BF16), dense_w_up[i].astype(BF16),
                     dense_w_down[i].astype(BF16), fin, last)
        else:
            wr = _pad_lanes(router_w[i])
            wr_hi = wr.astype(BF16)
            wr_lo = (wr - wr_hi.astype(F32)).astype(BF16)
            h = _moe(h, g_ffn, moe_w_gate[i].astype(BF16), moe_w_up[i].astype(BF16),
                     moe_w_down[i].astype(BF16), wr_hi, wr_lo, fin, last)
    return h.reshape(batch, seq, d)
```

```python
import functools

import jax
import jax.numpy as jnp
from jax import lax
from jax.experimental import pallas as pl
from jax.experimental.pallas import tpu as pltpu

F32 = jnp.float32
BF16 = jnp.bfloat16

HEAD_DIM = 128
LANES = 128
N_EXPERTS_TOP_K = 2
NORM_EPS = 1e-6
EXP_ZERO_BELOW = -104.0
NEG_BIG = -1e30
VMEM_LIMIT_BYTES = 56 * 1024 * 1024

ATTN_BLOCK = 256
ATTN_SUB_BLOCKS = 8
CUMSUM_BLOCK = 256
ROW_BLOCK = 512
IN_PROJ_ROW_BLOCK = 1024
FF_BLOCK = 512
EXPERT_TILE_ROWS = 512
DISPATCH_TOKENS = 512
COMBINE_TOKENS = 512
ROW_COPY_UNROLL = 8


def _params(*sem):
    return pltpu.CompilerParams(dimension_semantics=sem, vmem_limit_bytes=VMEM_LIMIT_BYTES)


def _row_rms_scale(x):
    return lax.rsqrt(jnp.mean(x * x, axis=-1, keepdims=True) + NORM_EPS)


def _in_proj_kernel(x_ref, g_ref, w_ref, wf_ref, qk_gain_ref, p_ref, f_ref, a_scr, *, scale):
    j = pl.program_id(1)

    @pl.when(j == 0)
    def _():
        x = x_ref[...]
        a = (x * _row_rms_scale(x)) * g_ref[...]
        a_scr[...] = a.astype(BF16)
        f_ref[...] = jnp.dot(a_scr[...], wf_ref[...], preferred_element_type=F32)

    acc = jnp.dot(a_scr[...], w_ref[...], preferred_element_type=F32)
    heads = acc.shape[1] // HEAD_DIM

    @pl.when(j == 0)
    def _():
        p_ref[...] = (acc * scale).astype(BF16)

    @pl.when(jnp.logical_or(j == 3, j == 4))
    def _():
        gain = qk_gain_ref[j - 3]
        for hh in range(heads):
            t = acc[:, hh * HEAD_DIM:(hh + 1) * HEAD_DIM]
            y = (t * _row_rms_scale(t)) * gain
            p_ref[:, hh * HEAD_DIM:(hh + 1) * HEAD_DIM] = y.astype(BF16)

    @pl.when(jnp.logical_and(j != 0, jnp.logical_and(j != 3, j != 4)))
    def _():
        p_ref[...] = acc.astype(BF16)


def _in_proj(h, g, w, wf, qk_gain, seg_width):
    n, d = h.shape
    bm = min(IN_PROJ_ROW_BLOCK, n)
    bn = seg_width
    n_tiles = w.shape[1] // bn
    kern = functools.partial(_in_proj_kernel, scale=HEAD_DIM ** -0.5)
    return pl.pallas_call(
        kern,
        grid=(n // bm, n_tiles),
        in_specs=[
            pl.BlockSpec((bm, d), lambda i, j: (i, 0)),
            pl.BlockSpec((1, d), lambda i, j: (0, 0)),
            pl.BlockSpec((d, bn), lambda i, j: (0, j)),
            pl.BlockSpec((d, LANES), lambda i, j: (0, 0)),
            pl.BlockSpec((2, 1, HEAD_DIM), lambda i, j: (0, 0, 0)),
        ],
        out_specs=[
            pl.BlockSpec((bm, bn), lambda i, j: (i, j)),
            pl.BlockSpec((bm, LANES), lambda i, j: (i, 0)),
        ],
        out_shape=[
            jax.ShapeDtypeStruct((n, w.shape[1]), BF16),
            jax.ShapeDtypeStruct((n, LANES), F32),
        ],
        scratch_shapes=[pltpu.VMEM((bm, d), BF16)],
        compiler_params=_params("parallel", "arbitrary"),
        name="in_proj",
    )(h, g, w, wf, qk_gain)


def _split3(x):
    hi = x.astype(BF16)
    r = x - hi.astype(F32)
    mid = r.astype(BF16)
    lo = (r - mid.astype(F32)).astype(BF16)
    return hi, mid, lo


def _forget_cumsum_kernel(f_ref, b_ref, c_ref, ct_ref, carry_scr, *, n_heads):
    @pl.when(pl.program_id(1) == 0)
    def _():
        carry_scr[...] = jnp.zeros_like(carry_scr)

    x = f_ref[...] + b_ref[...]
    log_f = jnp.minimum(x, 0.0) - jnp.log(1.0 + jnp.exp(-jnp.abs(x)))
    bc = x.shape[0]
    row = lax.broadcasted_iota(jnp.int32, (bc, bc), 0)
    col = lax.broadcasted_iota(jnp.int32, (bc, bc), 1)
    lower = jnp.where(col <= row, 1.0, 0.0).astype(BF16)
    hi, mid, lo = _split3(log_f)
    cum = (jnp.dot(lower, hi, preferred_element_type=F32)
           + jnp.dot(lower, mid, preferred_element_type=F32)
           + jnp.dot(lower, lo, preferred_element_type=F32))
    c = cum + carry_scr[...]
    c_ref[...] = c
    carry_scr[...] = c[bc - 1:bc, :]
    ct_ref[0] = c.T[:n_heads, :]


def _forget_cumsum(f, b_pad, batch, n_heads):
    n = f.shape[0]
    s = n // batch
    bc = min(CUMSUM_BLOCK, s)
    nblk = s // bc
    kern = functools.partial(_forget_cumsum_kernel, n_heads=n_heads)
    return pl.pallas_call(
        kern,
        grid=(batch, nblk),
        in_specs=[
            pl.BlockSpec((bc, LANES), lambda b, t: (b * nblk + t, 0)),
            pl.BlockSpec((1, LANES), lambda b, t: (0, 0)),
        ],
        out_specs=[
            pl.BlockSpec((bc, LANES), lambda b, t: (b * nblk + t, 0)),
            pl.BlockSpec((1, n_heads, bc), lambda b, t: (b, 0, t)),
        ],
        out_shape=[
            jax.ShapeDtypeStruct((n, LANES), F32),
            jax.ShapeDtypeStruct((batch, n_heads, s), F32),
        ],
        scratch_shapes=[pltpu.VMEM((1, LANES), F32)],
        compiler_params=_params("arbitrary", "arbitrary"),
        name="forget_cumsum",
    )(f, b_pad)


def _max_key_norm(k_ref, out_scr, chunk):
    n_chunks = k_ref.shape[0] // chunk

    def body(c, mx):
        kb = k_ref[pl.ds(pl.multiple_of(c * chunk, chunk), chunk), :].astype(F32)
        n2 = jnp.sum(kb * kb, axis=-1, keepdims=True)
        return jnp.maximum(mx, jnp.max(n2, axis=0, keepdims=True))

    mx = lax.fori_loop(0, n_chunks, body, jnp.zeros((1, 1), F32))
    out_scr[...] = jnp.broadcast_to(jnp.sqrt(mx), out_scr.shape)


def _qk(q, kb):
    return lax.dot_general(q, kb, (((1,), (1,)), ((), ())), preferred_element_type=F32)


def _head_rmsnorm(o, gain):
    return (o * _row_rms_scale(o)) * gain


def _softplus(z):
    return jnp.maximum(z, 0.0) + jnp.log(1.0 + jnp.exp(-jnp.abs(z)))


def _sb_attn_kernel(q_ref, k_ref, v_ref, tri_ref, gain_ref, o_ref,
                    acc_scr, carry_scr, kmax_scr, *, blk, sub, exit_below):
    qi = pl.program_id(2)

    @pl.when(qi == 0)
    def _():
        _max_key_norm(k_ref, kmax_scr, blk)

    tri = tri_ref[...]
    kmax = kmax_scr[0:1, 0:1]
    r_idx = lax.broadcasted_iota(jnp.int32, (blk, blk), 0)
    c_idx = lax.broadcasted_iota(jnp.int32, (blk, blk), 1)
    causal = c_idx < r_idx

    def kv_block(j):
        start = pl.multiple_of(j * blk, blk)
        return k_ref[pl.ds(start, blk), :], v_ref[pl.ds(start, blk), :]

    z_bounds, kvs, zs, sps, withins = [], [], [], [], []
    for r in range(sub):
        g = qi * sub + r
        q = q_ref[r * blk:(r + 1) * blk, :]
        qf = q.astype(F32)
        q_norm = jnp.sqrt(jnp.sum(qf * qf, axis=-1, keepdims=True))
        z_bounds.append(q_norm * kmax * 1.001 + 1e-3)
        kd, vd = kv_block(g)
        kp, vp = kv_block(jnp.maximum(g - 1, 0))
        kvs.append((vd, vp))
        zs.append((_qk(q, kd), _qk(q, kp)))
    for r in range(sub):
        z_d, z_p = zs[r]
        sps.append((jnp.where(causal, _softplus(z_d), 0.0), _softplus(z_p)))
    for r in range(sub):
        sp_d, sp_p = sps[r]
        withins.append((jnp.dot(sp_d.astype(BF16), tri, preferred_element_type=F32),
                        jnp.dot(sp_p.astype(BF16), tri, preferred_element_type=F32)))
    for r in range(sub):
        rows = slice(r * blk, (r + 1) * blk)
        (z_d, z_p), (sp_d, sp_p), (in_d, in_p), (vd, vp) = zs[r], sps[r], withins[r], kvs[r]
        carry = jnp.sum(sp_d, axis=-1, keepdims=True)
        w_d = jnp.where(causal, jnp.exp(z_d - in_d), 0.0)
        rev_base = carry if r > 0 else carry + jnp.where(qi > 0, 0.0, -NEG_BIG)
        w_p = jnp.exp(z_p - (in_p + rev_base))
        acc_scr[rows, :] = (jnp.dot(w_d.astype(BF16), vd, preferred_element_type=F32)
                            + jnp.dot(w_p.astype(BF16), vp, preferred_element_type=F32))
        carry_scr[rows, :] = carry + jnp.sum(sp_p, axis=-1, keepdims=True)

    def more_needed(it):
        worst = jnp.full((1, 1), NEG_BIG, F32)
        for r in range(sub):
            rows = slice(r * blk, (r + 1) * blk)
            slack = jnp.max(z_bounds[r] - carry_scr[rows, :], axis=0, keepdims=True)
            worst = jnp.maximum(worst, jnp.where(qi * sub + r - 2 - it >= 0, slack, NEG_BIG))
        return (jnp.max(worst) >= exit_below).astype(jnp.int32)

    def cond(state):
        _, go = state
        return go > 0

    def body(state):
        it, _ = state
        for r in range(sub):
            rows = slice(r * blk, (r + 1) * blk)
            j = qi * sub + r - 2 - it
            kb, vb = kv_block(jnp.maximum(j, 0))
            z = _qk(q_ref[rows, :], kb)
            sp = _softplus(z)
            within = jnp.dot(sp.astype(BF16), tri, preferred_element_type=F32)
            rev_base = carry_scr[rows, :] + jnp.where(j >= 0, 0.0, -NEG_BIG)
            w = jnp.exp(z - (within + rev_base))
            acc_scr[rows, :] += jnp.dot(w.astype(BF16), vb, preferred_element_type=F32)
            carry_scr[rows, :] += jnp.sum(sp, axis=-1, keepdims=True)
        return it + 1, more_needed(it + 1)

    lax.while_loop(cond, body, (0, more_needed(0)))
    o_ref[...] = _head_rmsnorm(acc_scr[...], gain_ref[0]).astype(BF16)


def _attn_tiling(s):
    blk = min(ATTN_BLOCK, s)
    sub = min(ATTN_SUB_BLOCKS, s // blk)
    return blk, sub, s // (blk * sub)


def _sb_attn(p, tri, gain, batch, n_heads, exit_below):
    n = p.shape[0]
    s = n // batch
    blk, sub, nq = _attn_tiling(s)
    tq = blk * sub
    kern = functools.partial(_sb_attn_kernel, blk=blk, sub=sub, exit_below=exit_below)
    return pl.pallas_call(
        kern,
        grid=(batch, n_heads, nq),
        in_specs=[
            pl.BlockSpec((tq, HEAD_DIM), lambda b, h, i: (b * nq + i, h)),
            pl.BlockSpec((s, HEAD_DIM), lambda b, h, i: (b, n_heads + h)),
            pl.BlockSpec((s, HEAD_DIM), lambda b, h, i: (b, 2 * n_heads + h)),
            pl.BlockSpec((blk, blk), lambda b, h, i: (0, 0)),
            pl.BlockSpec((1, 1, HEAD_DIM), lambda b, h, i: (h, 0, 0)),
        ],
        out_specs=pl.BlockSpec((tq, HEAD_DIM), lambda b, h, i: (b * nq + i, h)),
        out_shape=jax.ShapeDtypeStruct((n, n_heads * HEAD_DIM), BF16),
        scratch_shapes=[
            pltpu.VMEM((tq, HEAD_DIM), F32),
            pltpu.VMEM((tq, 1), F32),
            pltpu.VMEM((8, LANES), F32),
        ],
        compiler_params=_params("arbitrary", "arbitrary", "arbitrary"),
        name="sb_attn",
    )(p, p, p, tri, gain)


def _fox_attn_kernel(q_ref, k_ref, v_ref, gate_ref, c_ref, ct_ref, gain_ref, o_ref,
                     acc_scr, m_scr, l_scr, kmax_scr, *, blk, sub, exit_below):
    h = pl.program_id(1)
    qi = pl.program_id(2)

    @pl.when(qi == 0)
    def _():
        _max_key_norm(k_ref, kmax_scr, blk)

    kmax = kmax_scr[0:1, 0:1]
    r_idx = lax.broadcasted_iota(jnp.int32, (blk, blk), 0)
    c_idx = lax.broadcasted_iota(jnp.int32, (blk, blk), 1)
    causal = c_idx <= r_idx
    lane = lax.broadcasted_iota(jnp.int32, (blk, LANES), 1)

    def kv_block(j):
        start = pl.multiple_of(j * blk, blk)
        return k_ref[pl.ds(start, blk), :], v_ref[pl.ds(start, blk), :]

    bounds = []
    for r in range(sub):
        rows = slice(r * blk, (r + 1) * blk)
        g = qi * sub + r
        gp = jnp.maximum(g - 1, 0)
        q = q_ref[rows, :]
        qf = q.astype(F32)
        q_norm = jnp.sqrt(jnp.sum(qf * qf, axis=-1, keepdims=True))
        c_q = jnp.sum(jnp.where(lane == h, c_ref[rows, :], 0.0), axis=-1, keepdims=True)
        bounds.append((q_norm * kmax * 1.001 + 1.0) + c_q)
        kd, vd = kv_block(g)
        kp, vp = kv_block(gp)
        c_kd = ct_ref[0, g]
        c_kp = ct_ref[0, gp]
        logit_d = jnp.where(causal, _qk(q, kd) + (c_q - c_kd), -jnp.inf)
        c_q_prev = c_q if r > 0 else c_q + jnp.where(g > 0, 0.0, NEG_BIG)
        logit_p = _qk(q, kp) + (c_q_prev - c_kp)
        m = jnp.maximum(jnp.max(logit_d, axis=-1, keepdims=True),
                        jnp.max(logit_p, axis=-1, keepdims=True))
        pd = jnp.exp(logit_d - m)
        pp = jnp.exp(logit_p - m)
        m_scr[rows, :] = m
        l_scr[rows, :] = jnp.sum(pd, axis=-1, keepdims=True) + jnp.sum(pp, axis=-1, keepdims=True)
        acc_scr[rows, :] = (jnp.dot(pd.astype(BF16), vd, preferred_element_type=F32)
                            + jnp.dot(pp.astype(BF16), vp, preferred_element_type=F32))

    def more_needed(it):
        worst = jnp.full((1, 1), NEG_BIG, F32)
        for r in range(sub):
            rows = slice(r * blk, (r + 1) * blk)
            j = qi * sub + r - 2 - it
            first_c = ct_ref[0, jnp.maximum(j + 1, 0)][0:1, 0:1]
            slack = jnp.max(bounds[r] - first_c - m_scr[rows, :], axis=0, keepdims=True)
            worst = jnp.maximum(worst, jnp.where(j >= 0, slack, NEG_BIG))
        return (jnp.max(worst) >= exit_below).astype(jnp.int32)

    def cond(state):
        _, go = state
        return go > 0

    def body(state):
        it, _ = state
        for r in range(sub):
            rows = slice(r * blk, (r + 1) * blk)
            j = qi * sub + r - 2 - it
            jc = jnp.maximum(j, 0)
            kb, vb = kv_block(jc)
            c_q = jnp.sum(jnp.where(lane == h, c_ref[rows, :], 0.0), axis=-1, keepdims=True)
            c_q = c_q + jnp.where(j >= 0, 0.0, NEG_BIG)
            logit = _qk(q_ref[rows, :], kb) + (c_q - ct_ref[0, jc])
            m_old = m_scr[rows, :]
            m_new = jnp.maximum(m_old, jnp.max(logit, axis=-1, keepdims=True))
            alpha = jnp.exp(m_old - m_new)
            pr = jnp.exp(logit - m_new)
            l_scr[rows, :] = alpha * l_scr[rows, :] + jnp.sum(pr, axis=-1, keepdims=True)
            acc_scr[rows, :] = alpha * acc_scr[rows, :] + jnp.dot(pr.astype(BF16), vb,
                                                                  preferred_element_type=F32)
            m_scr[rows, :] = m_new
        return it + 1, more_needed(it + 1)

    lax.while_loop(cond, body, (0, more_needed(0)))
    o = acc_scr[...] / l_scr[...]
    y = _head_rmsnorm(o, gain_ref[0])
    gate = gate_ref[...].astype(F32)
    o_ref[...] = (y * (1.0 / (1.0 + jnp.exp(-gate)))).astype(BF16)


def _fox_attn(p, c, ct4, gain, batch, n_sb_heads, n_heads, exit_below):
    n = p.shape[0]
    s = n // batch
    blk, sub, nq = _attn_tiling(s)
    tq = blk * sub
    base = 3 * n_sb_heads
    kern = functools.partial(_fox_attn_kernel, blk=blk, sub=sub, exit_below=exit_below)
    return pl.pallas_call(
        kern,
        grid=(batch, n_heads, nq),
        in_specs=[
            pl.BlockSpec((tq, HEAD_DIM), lambda b, h, i: (b * nq + i, base + h)),
            pl.BlockSpec((s, HEAD_DIM), lambda b, h, i: (b, base + n_heads + h)),
            pl.BlockSpec((s, HEAD_DIM), lambda b, h, i: (b, base + 2 * n_heads + h)),
            pl.BlockSpec((tq, HEAD_DIM), lambda b, h, i: (b * nq + i, base + 3 * n_heads + h)),
            pl.BlockSpec((tq, LANES), lambda b, h, i: (b * nq + i, 0)),
            pl.BlockSpec((1, s // blk, 1, blk), lambda b, h, i: (b * n_heads + h, 0, 0, 0)),
            pl.BlockSpec((1, 1, HEAD_DIM), lambda b, h, i: (h, 0, 0)),
        ],
        out_specs=pl.BlockSpec((tq, HEAD_DIM), lambda b, h, i: (b * nq + i, h)),
        out_shape=jax.ShapeDtypeStruct((n, n_heads * HEAD_DIM), BF16),
        scratch_shapes=[
            pltpu.VMEM((tq, HEAD_DIM), F32),
            pltpu.VMEM((tq, 1), F32),
            pltpu.VMEM((tq, 1), F32),
            pltpu.VMEM((8, LANES), F32),
        ],
        compiler_params=_params("arbitrary", "arbitrary", "arbitrary"),
        name="fox_attn",
    )(p, p, p, p, c, ct4, gain)


def _out_proj_kernel(h_ref, a_ref, b_ref, wa_ref, wb_ref, o_ref):
    o_ref[...] = (h_ref[...]
                  + jnp.dot(a_ref[...], wa_ref[...], preferred_element_type=F32)
                  + jnp.dot(b_ref[...], wb_ref[...], preferred_element_type=F32))


def _out_proj(h, o_sb, o_fox, w_sb, w_fox):
    n, d = h.shape
    bm = min(ROW_BLOCK, n)
    return pl.pallas_call(
        _out_proj_kernel,
        grid=(n // bm,),
        in_specs=[
            pl.BlockSpec((bm, d), lambda i: (i, 0)),
            pl.BlockSpec((bm, o_sb.shape[1]), lambda i: (i, 0)),
            pl.BlockSpec((bm, o_fox.shape[1]), lambda i: (i, 0)),
            pl.BlockSpec(w_sb.shape, lambda i: (0, 0)),
            pl.BlockSpec(w_fox.shape, lambda i: (0, 0)),
        ],
        out_specs=pl.BlockSpec((bm, d), lambda i: (i, 0)),
        out_shape=jax.ShapeDtypeStruct((n, d), F32),
        compiler_params=_params("parallel"),
        name="out_proj",
    )(h, o_sb, o_fox, w_sb, w_fox)


def _router_gates(m32, m_hi, wr_hi_ref, wr_lo_ref, n_experts):
    m_lo = (m32 - m_hi.astype(F32)).astype(BF16)
    logits = (jnp.dot(m_hi, wr_hi_ref[...], preferred_element_type=F32)
              + jnp.dot(m_lo, wr_hi_ref[...], preferred_element_type=F32)
              + jnp.dot(m_hi, wr_lo_ref[...], preferred_element_type=F32))
    lane = lax.broadcasted_iota(jnp.int32, logits.shape, 1)
    valid = lane < n_experts
    logits = jnp.where(valid, logits, -jnp.inf)
    e = jnp.exp(logits - jnp.max(logits, axis=-1, keepdims=True))
    probs = e / jnp.sum(e, axis=-1, keepdims=True)
    p1 = jnp.max(probs, axis=-1, keepdims=True)
    i1 = jnp.min(jnp.where(probs == p1, lane, LANES), axis=-1, keepdims=True)
    rest = jnp.where(lane == i1, -1.0, probs)
    p2 = jnp.max(rest, axis=-1, keepdims=True)
    i2 = jnp.min(jnp.where(rest == p2, lane, LANES), axis=-1, keepdims=True)
    denom = p1 + p2
    first, second = lane == i1, lane == i2
    gates = jnp.where(first, p1 / denom, 0.0) + jnp.where(second, p2 / denom, 0.0)
    return gates, jnp.logical_or(first, second)


def _swiglu(m, wg, wu, wd):
    gate = jnp.dot(m, wg, preferred_element_type=F32)
    up = jnp.dot(m, wu, preferred_element_type=F32)
    act = ((gate * (1.0 / (1.0 + jnp.exp(-gate)))) * up).astype(BF16)
    return jnp.dot(act, wd, preferred_element_type=F32)


def _ffn_kernel(x_ref, g_ref, wg_ref, wu_ref, wd_ref, fin_ref, o_ref, m_scr, acc_scr, *, final_norm):
    j = pl.program_id(1)

    @pl.when(j == 0)
    def _():
        x = x_ref[...]
        m_scr[...] = ((x * _row_rms_scale(x)) * g_ref[...]).astype(BF16)
        acc_scr[...] = jnp.zeros_like(acc_scr)

    acc_scr[...] += _swiglu(m_scr[...], wg_ref[...], wu_ref[...], wd_ref[...])

    @pl.when(j == pl.num_programs(1) - 1)
    def _():
        out = x_ref[...] + acc_scr[...]
        if final_norm:
            out = (out * _row_rms_scale(out)) * fin_ref[...]
        o_ref[...] = out


def _ffn(h, g, wg, wu, wd, fin, final_norm):
    n, d = h.shape
    f = wg.shape[1]
    bm = min(ROW_BLOCK, n)
    bf = min(FF_BLOCK, f)
    return pl.pallas_call(
        functools.partial(_ffn_kernel, final_norm=final_norm),
        grid=(n // bm, f // bf),
        in_specs=[
            pl.BlockSpec((bm, d), lambda i, j: (i, 0)),
            pl.BlockSpec((1, d), lambda i, j: (0, 0)),
            pl.BlockSpec((d, bf), lambda i, j: (0, j)),
            pl.BlockSpec((d, bf), lambda i, j: (0, j)),
            pl.BlockSpec((bf, d), lambda i, j: (j, 0)),
            pl.BlockSpec((1, d), lambda i, j: (0, 0)),
        ],
        out_specs=pl.BlockSpec((bm, d), lambda i, j: (i, 0)),
        out_shape=jax.ShapeDtypeStruct((n, d), F32),
        scratch_shapes=[pltpu.VMEM((bm, d), BF16), pltpu.VMEM((bm, d), F32)],
        compiler_params=_params("parallel", "arbitrary"),
        name="dense_ffn",
    )(h, g, wg, wu, wd, fin)


def _route_kernel(x_ref, g_ref, wr_hi_ref, wr_lo_ref, gates_ref, sel_ref, *, n_experts):
    x = x_ref[...]
    m32 = (x * _row_rms_scale(x)) * g_ref[...]
    gates, chosen = _router_gates(m32, m32.astype(BF16), wr_hi_ref, wr_lo_ref, n_experts)
    gates_ref[...] = gates
    sel_ref[...] = jnp.where(chosen, 1.0, 0.0).astype(BF16)


def _route(h, g, wr_hi, wr_lo, n_experts):
    n, d = h.shape
    bm = min(ROW_BLOCK, n)
    return pl.pallas_call(
        functools.partial(_route_kernel, n_experts=n_experts),
        grid=(n // bm,),
        in_specs=[
            pl.BlockSpec((bm, d), lambda i: (i, 0)),
            pl.BlockSpec((1, d), lambda i: (0, 0)),
            pl.BlockSpec((d, LANES), lambda i: (0, 0)),
            pl.BlockSpec((d, LANES), lambda i: (0, 0)),
        ],
        out_specs=[pl.BlockSpec((bm, LANES), lambda i: (i, 0)),
                   pl.BlockSpec((bm, LANES), lambda i: (i, 0))],
        out_shape=[jax.ShapeDtypeStruct((n, LANES), F32),
                   jax.ShapeDtypeStruct((n, LANES), BF16)],
        compiler_params=_params("parallel"),
        name="moe_route",
    )(h, g, wr_hi, wr_lo)


PLAN_ROWS = 8


def _plan_kernel(sel_ref, info_ref, counts_ref, carry_scr):
    @pl.when(pl.program_id(0) == 0)
    def _():
        carry_scr[...] = jnp.zeros_like(carry_scr)

    sel = sel_ref[...]
    self32 = sel.astype(F32)
    bc = sel.shape[0]
    row = lax.broadcasted_iota(jnp.int32, (bc, bc), 0)
    col = lax.broadcasted_iota(jnp.int32, (bc, bc), 1)
    lower = jnp.where(col <= row, 1.0, 0.0).astype(BF16)
    cum = jnp.dot(lower, sel, preferred_element_type=F32) + carry_scr[...]
    rank = cum - self32
    lane = lax.broadcasted_iota(jnp.int32, sel.shape, 1)
    picked = self32 > 0.0
    e_lo = jnp.min(jnp.where(picked, lane, LANES), axis=-1, keepdims=True)
    e_hi = jnp.max(jnp.where(picked, lane, -1), axis=-1, keepdims=True)
    at_lo, at_hi = lane == e_lo, lane == e_hi

    def pick(mask, v):
        return jnp.sum(jnp.where(mask, v, 0.0), axis=-1, keepdims=True)

    cols = [e_lo.astype(F32), e_hi.astype(F32), pick(at_lo, rank), pick(at_hi, rank)]
    tile = jnp.zeros(sel.shape, F32)
    for idx, cval in enumerate(cols):
        tile = jnp.where(lane == idx, cval, tile)
    info_ref[...] = tile.T[:PLAN_ROWS, :]
    carry_scr[...] = cum[bc - 1:bc, :]
    counts_ref[...] = cum[bc - 1:bc, :]


def _plan(sel):
    n = sel.shape[0]
    bc = min(CUMSUM_BLOCK, n)
    return pl.pallas_call(
        _plan_kernel,
        grid=(n // bc,),
        in_specs=[pl.BlockSpec((bc, LANES), lambda t: (t, 0))],
        out_specs=[pl.BlockSpec((PLAN_ROWS, bc), lambda t: (0, t)),
                   pl.BlockSpec((1, LANES), lambda t: (0, 0))],
        out_shape=[jax.ShapeDtypeStruct((PLAN_ROWS, n), F32),
                   jax.ShapeDtypeStruct((1, LANES), F32)],
        scratch_shapes=[pltpu.VMEM((1, LANES), F32)],
        compiler_params=_params("arbitrary"),
        name="moe_plan",
    )(sel)


def _row_copy(src_ref, src_row, dst_ref, dst_row, sem):
    return pltpu.make_async_copy(src_ref.at[pl.ds(src_row, 1), :], dst_ref.at[pl.ds(dst_row, 1), :], sem)


def _dispatch_kernel(dst_ref, h_ref, xs_in_ref, xs_ref, sem, *, bt):
    del xs_in_ref

    def issue(t, carry):
        _row_copy(h_ref, t, xs_ref, dst_ref[0, t], sem).start()
        _row_copy(h_ref, t, xs_ref, dst_ref[1, t], sem).start(priority=1)
        return carry

    lax.fori_loop(0, bt, issue, 0, unroll=ROW_COPY_UNROLL)
    for _ in range(N_EXPERTS_TOP_K):
        pltpu.make_async_copy(h_ref, xs_ref.at[pl.ds(0, bt), :], sem).wait()


def _dispatch(h, dst, xs_zero):
    n, d = h.shape
    bt = min(DISPATCH_TOKENS, n)
    return pl.pallas_call(
        functools.partial(_dispatch_kernel, bt=bt),
        grid=(n // bt,),
        in_specs=[
            pl.BlockSpec((2, bt), lambda i: (0, i), memory_space=pltpu.SMEM),
            pl.BlockSpec((bt, d), lambda i: (i, 0)),
            pl.BlockSpec(memory_space=pl.ANY),
        ],
        out_specs=pl.BlockSpec(memory_space=pl.ANY),
        out_shape=jax.ShapeDtypeStruct(xs_zero.shape, xs_zero.dtype),
        scratch_shapes=[pltpu.SemaphoreType.DMA],
        input_output_aliases={2: 0},
        compiler_params=_params("arbitrary"),
        name="moe_dispatch",
    )(dst, h, xs_zero)


def _experts_kernel(tile_expert_ref, n_active_ref, x_ref, g_ref, wg_ref, wu_ref, wd_ref, y_ref):
    del tile_expert_ref
    i = pl.program_id(0)

    @pl.when(i < n_active_ref[0])
    def _():
        x = x_ref[...]
        m = ((x * _row_rms_scale(x)) * g_ref[...]).astype(BF16)
        y_ref[...] = _swiglu(m, wg_ref[...], wu_ref[...], wd_ref[...])

    @pl.when(i >= n_active_ref[0])
    def _():
        y_ref[...] = jnp.zeros_like(y_ref)


def _experts(xs, g, wg, wu, wd, tile_expert, n_active, tile_rows):
    r, d = xs.shape
    _, _, f = wg.shape
    n_tiles = r // tile_rows
    return pl.pallas_call(
        _experts_kernel,
        grid_spec=pltpu.PrefetchScalarGridSpec(
            num_scalar_prefetch=2,
            grid=(n_tiles,),
            in_specs=[
                pl.BlockSpec((tile_rows, d), lambda i, te, na: (i, 0)),
                pl.BlockSpec((1, d), lambda i, te, na: (0, 0)),
                pl.BlockSpec((None, d, f), lambda i, te, na: (te[i], 0, 0)),
                pl.BlockSpec((None, d, f), lambda i, te, na: (te[i], 0, 0)),
                pl.BlockSpec((None, f, d), lambda i, te, na: (te[i], 0, 0)),
            ],
            out_specs=pl.BlockSpec((tile_rows, d), lambda i, te, na: (i, 0)),
        ),
        out_shape=jax.ShapeDtypeStruct((r, d), F32),
        compiler_params=_params("arbitrary"),
        name="moe_experts",
    )(tile_expert, n_active, xs, g, wg, wu, wd)


def _combine_kernel(dst_ref, h_ref, gates_ref, sel_ref, ys_ref, fin_ref, o_ref, y_lo, y_hi, sem, *,
                    bt, final_norm):
    def issue(t, carry):
        _row_copy(ys_ref, dst_ref[0, t], y_lo, t, sem).start()
        _row_copy(ys_ref, dst_ref[1, t], y_hi, t, sem).start(priority=1)
        return carry

    lax.fori_loop(0, bt, issue, 0, unroll=ROW_COPY_UNROLL)
    gates = gates_ref[...]
    picked = sel_ref[...].astype(F32) > 0.0
    lane = lax.broadcasted_iota(jnp.int32, gates.shape, 1)
    e_lo = jnp.min(jnp.where(picked, lane, LANES), axis=-1, keepdims=True)
    e_hi = jnp.max(jnp.where(picked, lane, -1), axis=-1, keepdims=True)
    g_lo = jnp.sum(jnp.where(lane == e_lo, gates, 0.0), axis=-1, keepdims=True)
    g_hi = jnp.sum(jnp.where(lane == e_hi, gates, 0.0), axis=-1, keepdims=True)
    pltpu.make_async_copy(ys_ref.at[pl.ds(0, bt), :], y_lo, sem).wait()
    pltpu.make_async_copy(ys_ref.at[pl.ds(0, bt), :], y_hi, sem).wait()
    out = h_ref[...] + g_lo * y_lo[...] + g_hi * y_hi[...]
    if final_norm:
        out = (out * _row_rms_scale(out)) * fin_ref[...]
    o_ref[...] = out


def _combine(h, dst, gates, sel, ys, fin, final_norm):
    n, d = h.shape
    bt = min(COMBINE_TOKENS, n)
    return pl.pallas_call(
        functools.partial(_combine_kernel, bt=bt, final_norm=final_norm),
        grid=(n // bt,),
        in_specs=[
            pl.BlockSpec((2, bt), lambda i: (0, i), memory_space=pltpu.SMEM),
            pl.BlockSpec((bt, d), lambda i: (i, 0)),
            pl.BlockSpec((bt, LANES), lambda i: (i, 0)),
            pl.BlockSpec((bt, LANES), lambda i: (i, 0)),
            pl.BlockSpec(memory_space=pl.ANY),
            pl.BlockSpec((1, d), lambda i: (0, 0)),
        ],
        out_specs=pl.BlockSpec((bt, d), lambda i: (i, 0)),
        out_shape=jax.ShapeDtypeStruct((n, d), F32),
        scratch_shapes=[pltpu.VMEM((bt, d), F32), pltpu.VMEM((bt, d), F32),
                        pltpu.SemaphoreType.DMA],
        compiler_params=_params("arbitrary"),
        name="moe_combine",
    )(dst, h, gates, sel, ys, fin)


def _moe(h, g, wg, wu, wd, wr_hi, wr_lo, fin, final_norm):
    n, d = h.shape
    n_experts = wg.shape[0]
    tile_rows = min(EXPERT_TILE_ROWS, n)
    gates, sel = _route(h, g, wr_hi, wr_lo, n_experts)
    info, counts = _plan(sel)
    counts = counts[0, :n_experts].astype(jnp.int32)
    padded = ((counts + tile_rows - 1) // tile_rows) * tile_rows
    ends = jnp.cumsum(padded)
    starts = ends - padded
    e_lo, e_hi = info[0].astype(jnp.int32), info[1].astype(jnp.int32)
    dst = jnp.stack([starts[e_lo] + info[2].astype(jnp.int32),
                     starts[e_hi] + info[3].astype(jnp.int32)])
    n_tiles = (N_EXPERTS_TOP_K * n) // tile_rows + n_experts
    tile_start = jnp.arange(n_tiles, dtype=jnp.int32) * tile_rows
    tile_expert = jnp.minimum(jnp.sum(tile_start[:, None] >= ends[None, :], axis=1),
                              n_experts - 1).astype(jnp.int32)
    n_active = (ends[-1] // tile_rows).reshape(1).astype(jnp.int32)
    xs = _dispatch(h, dst, jnp.zeros((n_tiles * tile_rows, d), F32))
    ys = _experts(xs, g, wg, wu, wd, tile_expert, n_active, tile_rows)
    return _combine(h, dst, gates, sel, ys, fin, final_norm)


def _pad_lanes(a):
    return jnp.pad(a, ((0, 0),) * (a.ndim - 1) + ((0, LANES - a.shape[-1]),))


def kernel(x, attn_norm, w_in, b_forget, fox_q_norm, fox_k_norm, sb_out_norm, fox_out_norm,
           w_out, ffn_norm, dense_w_gate, dense_w_up, dense_w_down, router_w, moe_w_gate,
           moe_w_up, moe_w_down, final_norm):
    batch, seq, d = x.shape
    depth = w_in.shape[0]
    n_fox = b_forget.shape[1]
    fox_width = n_fox * HEAD_DIM
    sb_width = sb_out_norm.shape[1]
    n_sb = sb_width // HEAD_DIM
    assert sb_width == fox_width, "projection column tiles assume equal head-group widths"
    assert w_in.shape[2] == 3 * sb_width + 4 * fox_width + n_fox
    n_main = 3 * sb_width + 4 * fox_width
    blk = min(ATTN_BLOCK, seq)
    assert seq % blk == 0 and n_fox <= 8 and depth >= 1
    scale = HEAD_DIM ** -0.5

    r = lax.broadcasted_iota(jnp.int32, (blk, blk), 0)
    c = lax.broadcasted_iota(jnp.int32, (blk, blk), 1)
    tri = (r >= c).astype(BF16)

    h = x.reshape(batch * seq, d)
    for layer in range(depth):
        w_main = w_in[layer, :, :n_main].astype(BF16)
        w_forget = _pad_lanes(w_in[layer, :, n_main:]).astype(BF16)
        qk_gain = jnp.stack([fox_q_norm[layer] * scale, fox_k_norm[layer]]).reshape(2, 1, HEAD_DIM)
        p, f = _in_proj(h, attn_norm[layer].reshape(1, d), w_main, w_forget, qk_gain, sb_width)
        c_tok, c_seq = _forget_cumsum(f, _pad_lanes(b_forget[layer].reshape(1, n_fox)), batch, n_fox)
        ct4 = c_seq.reshape(batch * n_fox, seq // blk, 1, blk)
        o_sb = _sb_attn(p, tri, sb_out_norm[layer].reshape(n_sb, 1, HEAD_DIM), batch, n_sb,
                        EXP_ZERO_BELOW)
        o_fox = _fox_attn(p, c_tok, ct4, fox_out_norm[layer].reshape(n_fox, 1, HEAD_DIM), batch,
                          n_sb, n_fox, EXP_ZERO_BELOW)
        w_o = w_out[layer].astype(BF16)
        h = _out_proj(h, o_sb, o_fox, w_o[:sb_width], w_o[sb_width:])
        i = layer // 2
        last = layer == depth - 1
        fin = final_norm.reshape(1, d)
        g_ffn = ffn_norm[layer].reshape(1, d)
        if layer % 2 == 0:
            h = _ffn(h, g_ffn, dense_w_gate[i].astype(BF16), dense_w_up[i].astype(BF16),
                     dense_w_down[i].astype(BF16), fin, last)
        else:
            wr = _pad_lanes(router_w[i])
            wr_hi = wr.astype(BF16)
            wr_lo = (wr - wr_hi.astype(F32)).astype(BF16)
            h = _moe(h, g_ffn, moe_w_gate[i].astype(BF16), moe_w_up[i].astype(BF16),
                     moe_w_down[i].astype(BF16), wr_hi, wr_lo, fin, last)
    return h.reshape(batch, seq, d)
```
